```python
import math
import jax, jax.numpy as jnp
from jax import lax
import numpy as np

D_MODEL = 2048
BATCH = 4
SEQ = 2048
DEPTH = 4
DEC_BATCH = 32
DEC_SEQ = 4
PAST_LEN = 16384
PAGE_SIZE = 128

F32 = jnp.float32
N_EVEN = (DEPTH + 1) // 2
N_ODD = DEPTH // 2
ALPHA = (2.0 * DEPTH) ** 0.25
BETA = (8.0 * DEPTH) ** -0.25
LN_EPS = 1e-5
RMS_EPS = 1e-5
ROPE_THETA = 10000.0
ATTN_BLOCK = 128

HD_A = 64
H_A = D_MODEL // (2 * HD_A)
KV_A = H_A // 8
WINDOW_A = 128

HD_B = 64
D_INNER = D_MODEL // 2
H_B = D_INNER // HD_B
NG_B = 2
D_STATE = 128
CONV_K = 4
CONV_DIM = D_INNER + 2 * NG_B * D_STATE
SSD_CHUNK = 128

EVEN_SIZES = (H_A * HD_A, KV_A * HD_A, KV_A * HD_A, D_INNER, CONV_DIM, H_B)
EVEN_SPLITS = tuple(sum(EVEN_SIZES[:i + 1]) for i in range(len(EVEN_SIZES) - 1))
EVEN_IN = sum(EVEN_SIZES)
EVEN_MIX = H_A * HD_A + D_INNER

HD_C = 128
H_C = D_MODEL // HD_C
KV_C = 4
DILATIONS = ((128, 1), (512, 4), (2048, 16))
N_DIL = len(DILATIONS)
WINDOW_C_MAX = max(w for w, _ in DILATIONS)
ODD_SPLITS = (N_DIL * H_C * HD_C, N_DIL * H_C * HD_C + KV_C * HD_C)
ODD_IN = N_DIL * H_C * HD_C + 2 * KV_C * HD_C

N_EXPERTS = 64
N_EXPERT_GROUPS = 8
TOPK_GROUPS = 4
TOP_K = 8
D_EXPERT = D_MODEL // 4
ROUTED_SCALE = 2.5
MOE_BLOCK = 64

PLE_DIM = 256

kernel_name = 'hybrid_swa_ssd_dilated_moe_step'


def _layernorm(x, g, b):
    xf = x.astype(F32)
    mu = jnp.mean(xf, -1, keepdims=True)
    xc = xf - mu
    var = jnp.mean(xc * xc, -1, keepdims=True)
    return (xc * lax.rsqrt(var + LN_EPS) * g + b).astype(x.dtype)


def _rope(x, pos):
    half = x.shape[-1] // 2
    inv = ROPE_THETA ** (-jnp.arange(half, dtype=F32) / half)
    ang = pos.astype(F32)[:, None] * inv
    shape = (pos.shape[0],) + (1,) * (x.ndim - 3) + (half,)
    cos, sin = jnp.cos(ang).reshape(shape), jnp.sin(ang).reshape(shape)
    xf = x.astype(F32)
    x1, x2 = xf[..., :half], xf[..., half:]
    return jnp.concatenate([x1 * cos - x2 * sin, x2 * cos + x1 * sin], -1).astype(x.dtype)


def _attend(q, k, v, mask, sink=None):
    scale = q.shape[-1] ** -0.5
    s = jnp.einsum('...qhgd,...shd->...hgqs', q.astype(F32), k.astype(F32)) * scale
    s = jnp.where(mask[..., None, None, :, :], s, -jnp.inf)
    m = jnp.max(s, -1, keepdims=True)
    if sink is not None:
        m = jnp.maximum(m, sink[:, :, None, None])
    m = jnp.where(jnp.isfinite(m), m, 0.0)
    e = jnp.exp(s - m)
    den = jnp.sum(e, -1)
    if sink is not None:
        den = den + jnp.exp(sink[:, :, None] - m[..., 0])
    o = jnp.einsum('...hgqs,...shd->...qhgd', e, v.astype(F32))
    o = o / jnp.moveaxis(den, -1, -3)[..., None]
    lse = m[..., 0] + jnp.log(den)
    return o.astype(q.dtype), lse


def _banded_attn(q, k, v, window, sink=None):
    B, L, H, D = q.shape
    KV = k.shape[2]
    blk = ATTN_BLOCK
    nb = -(-L // blk)
    pad = nb * blk - L
    padt = lambda t: jnp.pad(t, ((0, 0), (0, pad), (0, 0), (0, 0)))
    qb = padt(q).reshape(B, nb, blk, KV, H // KV, D)
    kb = padt(k).reshape(B, nb, blk, KV, D)
    vb = padt(v).reshape(B, nb, blk, KV, D)
    shift = lambda t: jnp.concatenate([jnp.zeros_like(t[:, :1]), t[:, :-1]], axis=1)
    kk = jnp.concatenate([shift(kb), kb], axis=2)
    vv = jnp.concatenate([shift(vb), vb], axis=2)
    base = jnp.arange(nb)[:, None] * blk
    qpos = base + jnp.arange(blk)
    kpos = base - blk + jnp.arange(2 * blk)
    dist = qpos[:, :, None] - kpos[:, None, :]
    mask = (dist >= 0) & (dist <= window) & (kpos[:, None, :] >= 0)
    o, lse = _attend(qb, kk, vv, mask, sink)
    o = o.reshape(B, nb * blk, H, D)[:, :L]
    lse = jnp.moveaxis(lse, -1, 2).reshape(B, nb * blk, H)[:, :L]
    return o, lse


def _window_attn_sample(q, k_new, v_new, buf_k, buf_v, pos0, window, sink):
    DB, T, H, D = q.shape
    KV = k_new.shape[2]
    lb = buf_k.shape[1]
    kk = jnp.concatenate([buf_k.astype(k_new.dtype), k_new], 1)
    vv = jnp.concatenate([buf_v.astype(v_new.dtype), v_new], 1)
    qpos = pos0 + jnp.arange(T)
    kpos = pos0 - lb + jnp.arange(lb + T)
    dist = qpos[:, None] - kpos[None, :]
    mask = (dist >= 0) & (dist <= window)
    o, _ = _attend(q.reshape(DB, T, KV, H // KV, D), kk, vv, mask, sink)
    return o.reshape(DB, T, H, D)


def _dilated_attn_prompt(q, k, v, window, dil):
    B, L, H, D = q.shape
    n = L // dil
    fold = lambda t: t.reshape(B, n, dil, *t.shape[2:]).swapaxes(1, 2).reshape(B * dil, n, *t.shape[2:])
    o, lse = _banded_attn(fold(q), fold(k), fold(v), window // dil)
    o = o.reshape(B, dil, n, H, D).swapaxes(1, 2).reshape(B, L, H, D)
    lse = lse.reshape(B, dil, n, H).swapaxes(1, 2).reshape(B, L, H)
    return o, lse


def _dilated_attn_sample(q, k_new, v_new, buf_k, buf_v, window, dil):
    DB, T, H, D = q.shape
    KV = k_new.shape[2]
    lb = buf_k.shape[1]
    kk = jnp.concatenate([buf_k.astype(k_new.dtype), k_new], 1)
    vv = jnp.concatenate([buf_v.astype(v_new.dtype), v_new], 1)
    n_keys = window // dil + 1
    idx = lb + jnp.arange(T)[:, None] - dil * jnp.arange(n_keys)[None, :]
    valid = idx >= 0
    idx = jnp.maximum(idx, 0)
    kg, vg = kk[:, idx], vv[:, idx]
    o, lse = _attend(q.reshape(DB, T, 1, KV, H // KV, D), kg, vg, valid[:, None, :])
    return o.reshape(DB, T, H, D), lse.reshape(DB, T, H)


def _causal_conv(u, u_prev, w, b):
    L = u.shape[1]
    full = jnp.concatenate([u_prev.astype(u.dtype), u], axis=1)
    y = sum(full[:, j:j + L] * w[j] for j in range(CONV_K)) + b
    return y, full[:, L:]


def _ssd(x, dt, a, bm, cm, h0):
    b, l, h, p = x.shape
    g, n = bm.shape[2], bm.shape[3]
    r = h // g
    q = min(SSD_CHUNK, l)
    pad = (-l) % q
    nc = (l + pad) // q
    padl = lambda t: jnp.pad(t.astype(F32), [(0, 0), (0, pad)] + [(0, 0)] * (t.ndim - 2))
    xr = padl(x).reshape(b, nc, q, g, r, p)
    dtr = padl(dt).reshape(b, nc, q, g, r)
    br = padl(bm).reshape(b, nc, q, g, n)
    cr = padl(cm).reshape(b, nc, q, g, n)
    acum = jnp.cumsum(dtr * a.astype(F32).reshape(g, r), axis=2)
    xdt = xr * dtr[..., None]
    causal = jnp.tril(jnp.ones((q, q), bool))
    seg = acum[:, :, :, None] - acum[:, :, None, :]
    decay = jnp.exp(jnp.where(causal[:, :, None, None], seg, -jnp.inf))
    cb = jnp.einsum('bclgn,bcsgn->bclsg', cr, br)
    y_diag = jnp.einsum('bclsg,bclsgr,bcsgrp->bclgrp', cb, decay, xdt)
    to_end = jnp.exp(acum[:, :, -1:] - acum)
    chunk_states = jnp.einsum('bcsgn,bcsgr,bcsgrp->bcgrpn', br, to_end, xdt)
    chunk_decay = jnp.exp(acum[:, :, -1])

    def step(hc, inp):
        st, dec = inp
        return hc * dec[..., None, None] + st, hc

    h_last, h_prev = lax.scan(step, h0.astype(F32).reshape(b, g, r, p, n),
                              (jnp.moveaxis(chunk_states, 1, 0), jnp.moveaxis(chunk_decay, 1, 0)))
    h_prev = jnp.moveaxis(h_prev, 0, 1)
    y_off = jnp.einsum('bclgn,bcgrpn,bclgr->bclgrp', cr, h_prev, jnp.exp(acum))
    y = (y_diag + y_off).reshape(b, nc * q, h, p)[:, :l]
    return y.astype(x.dtype), h_last.reshape(b, h, p, n).astype(h0.dtype)


def _gated_rmsnorm(y, z, w):
    B, L, C = y.shape
    u = (y.astype(F32) * jax.nn.silu(z.astype(F32))).reshape(B, L, NG_B, C // NG_B)
    u = u * lax.rsqrt(jnp.mean(u * u, -1, keepdims=True) + RMS_EPS)
    return (u.reshape(B, L, C) * w).astype(y.dtype)


def _even_mixer(x, pos0, win_k, win_v, ssm0, conv0, w_in, sink, conv_w, conv_b, dt_bias, a_log, d_skip, norm_w, w_out):
    B, L, _ = x.shape
    pos = pos0 + jnp.arange(L)
    q, k, v, z, xbc, dt = jnp.split(x @ w_in, EVEN_SPLITS, axis=-1)
    q = _rope(q.reshape(B, L, H_A, HD_A), pos)
    k = _rope(k.reshape(B, L, KV_A, HD_A), pos)
    v = v.reshape(B, L, KV_A, HD_A)
    sink_g = sink.astype(F32).reshape(KV_A, H_A // KV_A)
    if win_k is None:
        o_a, _ = _banded_attn(q, k, v, WINDOW_A, sink_g)
        keep = min(WINDOW_A, L)
        new_k, new_v = k[:, L - keep:], v[:, L - keep:]
    else:
        o_a = _window_attn_sample(q, k, v, win_k, win_v, pos0, WINDOW_A, sink_g)
        new_k, new_v = k, v
    u, conv_new = _causal_conv(xbc, conv0, conv_w, conv_b)
    u = jax.nn.silu(u)
    xs, bm, cm = jnp.split(u, (D_INNER, D_INNER + NG_B * D_STATE), axis=-1)
    xs = xs.reshape(B, L, H_B, HD_B)
    dtv = jax.nn.softplus(dt.astype(F32) + dt_bias.astype(F32))
    y, ssm_new = _ssd(xs, dtv, -jnp.exp(a_log.astype(F32)),
                      bm.reshape(B, L, NG_B, D_STATE), cm.reshape(B, L, NG_B, D_STATE), ssm0)
    y = y + xs * d_skip[:, None]
    y = _gated_rmsnorm(y.reshape(B, L, D_INNER), z, norm_w)
    out = jnp.concatenate([o_a.reshape(B, L, H_A * HD_A), y], -1) @ w_out
    return out, (new_k, new_v, ssm_new, conv_new)


def _odd_mixer(x, pos0, buf_k, buf_v, w_in, w_out):
    B, L, _ = x.shape
    pos = pos0 + jnp.arange(L)
    q, k, v = jnp.split(x @ w_in, ODD_SPLITS, axis=-1)
    q = _rope(q.reshape(B, L, N_DIL, H_C, HD_C), pos)
    k = _rope(k.reshape(B, L, KV_C, HD_C), pos)
    v = v.reshape(B, L, KV_C, HD_C)
    outs, lses = [], []
    for g, (win, dil) in enumerate(DILATIONS):
        if buf_k is None:
            o, lse = _dilated_attn_prompt(q[:, :, g], k, v, win, dil)
        else:
            o, lse = _dilated_attn_sample(q[:, :, g], k, v, buf_k, buf_v, win, dil)
        outs.append(o)
        lses.append(lse)
    alpha = jax.nn.softmax(jnp.stack(lses), axis=0)
    o = jnp.einsum('gblh,gblhd->blhd', alpha, jnp.stack(outs).astype(F32)).astype(x.dtype)
    out = o.reshape(B, L, H_C * HD_C) @ w_out
    if buf_k is None:
        keep = min(WINDOW_C_MAX, L)
        new_k, new_v = k[:, L - keep:], v[:, L - keep:]
    else:
        new_k, new_v = k, v
    return out, (new_k, new_v)


def _routed_experts(xt, idx, gate, w1, w3, w2):
    T, D = xt.shape
    n_exp = w1.shape[0]
    n_assign = idx.shape[0] * idx.shape[1]
    flat_e = idx.reshape(-1)
    flat_tok = jnp.repeat(jnp.arange(T, dtype=jnp.int32), idx.shape[1])
    flat_g = gate.reshape(-1)
    order = jnp.argsort(flat_e)
    e_sorted = flat_e[order]
    counts = jnp.bincount(flat_e, length=n_exp)
    padded = (counts + MOE_BLOCK - 1) // MOE_BLOCK * MOE_BLOCK
    start = jnp.cumsum(counts) - counts
    pad_end = jnp.cumsum(padded)
    pad_start = pad_end - padded
    dest = pad_start[e_sorted] + jnp.arange(n_assign) - start[e_sorted]
    n_blocks = -(-n_assign // MOE_BLOCK) + n_exp
    n_slots = n_blocks * MOE_BLOCK
    slot_tok = jnp.full((n_slots,), T, jnp.int32).at[dest].set(flat_tok[order])
    slot_g = jnp.zeros((n_slots,), xt.dtype).at[dest].set(flat_g[order].astype(xt.dtype))
    block_e = jnp.minimum(jnp.searchsorted(pad_end, jnp.arange(n_blocks) * MOE_BLOCK, side='right'), n_exp - 1)
    x_pad = jnp.concatenate([xt, jnp.zeros((1, D), xt.dtype)], 0)

    def block_ffn(args):
        tok, e = args
        xb = x_pad[tok]
        hb = jax.nn.silu(xb @ w1[e]) * (xb @ w3[e])
        return hb @ w2[e]

    y_blocks = lax.map(block_ffn, (slot_tok.reshape(n_blocks, MOE_BLOCK), block_e))
    y = jnp.zeros((T + 1, D), xt.dtype).at[slot_tok].add(y_blocks.reshape(n_slots, D) * slot_g[:, None])
    return y[:T]


def _moe(x, router_w, router_bias, w1, w3, w2, w1s, w3s, w2s):
    B, L, D = x.shape
    T = B * L
    xt = x.reshape(T, D)
    scores = jax.nn.sigmoid((xt @ router_w).astype(F32))
    choice = scores + router_bias.astype(F32)
    per_group = N_EXPERTS // N_EXPERT_GROUPS
    group_score = lax.top_k(choice.reshape(T, N_EXPERT_GROUPS, per_group), 2)[0].sum(-1)
    _, top_groups = lax.top_k(group_score, TOPK_GROUPS)
    gmask = jax.nn.one_hot(top_groups, N_EXPERT_GROUPS, dtype=F32).sum(1) > 0
    emask = jnp.repeat(gmask, per_group, axis=1)
    _, idx = lax.top_k(jnp.where(emask, choice, -jnp.inf), TOP_K)
    gate = jnp.take_along_axis(scores, idx, axis=-1)
    gate = gate / jnp.sum(gate, -1, keepdims=True) * ROUTED_SCALE
    routed = _routed_experts(xt, idx, gate, w1, w3, w2)
    shared = (jax.nn.silu(xt @ w1s) * (xt @ w3s)) @ w2s
    return (routed + shared).reshape(B, L, D)


def _finish_layer(x, h, p, ln_g, ln_b, router_w, router_bias, w1, w3, w2, w1s, w3s, w2s, w_gate, w_proj):
    x = _layernorm(ALPHA * x + h, ln_g[0], ln_b[0])
    x = _layernorm(ALPHA * x + _moe(x, router_w, router_bias, w1, w3, w2, w1s, w3s, w2s), ln_g[1], ln_b[1])
    return x + jax.nn.sigmoid(x @ w_gate) * (p @ w_proj)


def setup_inputs(seed: int = 0) -> dict:
    key = jax.random.key(seed)
    ks = iter(jax.random.split(key, 48))

    def nrm(shape, scale=1.0):
        return jax.random.normal(next(ks), shape, F32) * scale

    la = min(WINDOW_A, PAST_LEN)
    lc = min(WINDOW_C_MAX, PAST_LEN)
    dt0 = jnp.exp(jax.random.uniform(next(ks), (N_EVEN, H_B), F32, math.log(1e-3), math.log(1e-1)))
    a0 = jax.random.uniform(next(ks), (N_EVEN, H_B), F32, 1.0, 16.0)
    return {
        'x_prompt': nrm((BATCH, SEQ, D_MODEL)),
        'x_sample': nrm((DEC_BATCH, DEC_SEQ, D_MODEL)),
        'cache_a_k': nrm((N_EVEN, DEC_BATCH, la, KV_A, HD_A)),
        'cache_a_v': nrm((N_EVEN, DEC_BATCH, la, KV_A, HD_A)),
        'state_b_ssm': nrm((N_EVEN, DEC_BATCH, H_B, HD_B, D_STATE), 0.1),
        'state_b_conv': nrm((N_EVEN, DEC_BATCH, CONV_K - 1, CONV_DIM)),
        'cache_c_k': nrm((N_ODD, DEC_BATCH, lc, KV_C, HD_C)),
        'cache_c_v': nrm((N_ODD, DEC_BATCH, lc, KV_C, HD_C)),
        'p_prompt': nrm((DEPTH, BATCH, SEQ, PLE_DIM)),
        'p_sample': nrm((DEPTH, DEC_BATCH, DEC_SEQ, PLE_DIM)),
        'w_in_even': nrm((N_EVEN, D_MODEL, EVEN_IN), D_MODEL ** -0.5),
        'sink_a': nrm((N_EVEN, H_A), 0.5),
        'conv_w_b': nrm((N_EVEN, CONV_K, CONV_DIM), CONV_K ** -0.5),
        'conv_b_b': nrm((N_EVEN, CONV_DIM), 0.01),
        'dt_bias_b': dt0 + jnp.log(-jnp.expm1(-dt0)),
        'a_log_b': jnp.log(a0),
        'd_skip_b': 1.0 + nrm((N_EVEN, H_B), 0.1),
        'norm_w_b': 1.0 + nrm((N_EVEN, D_INNER), 0.02),
        'w_out_even': nrm((N_EVEN, EVEN_MIX, D_MODEL), EVEN_MIX ** -0.5 * BETA),
        'w_in_odd': nrm((N_ODD, D_MODEL, ODD_IN), D_MODEL ** -0.5),
        'w_out_odd': nrm((N_ODD, H_C * HD_C, D_MODEL), (H_C * HD_C) ** -0.5 * BETA),
        'ln_g': 1.0 + nrm((DEPTH, 2, D_MODEL), 0.02),
        'ln_b': nrm((DEPTH, 2, D_MODEL), 0.02),
        'router_w': nrm((DEPTH, D_MODEL, N_EXPERTS), D_MODEL ** -0.5),
        'router_bias': nrm((DEPTH, N_EXPERTS), 0.01),
        'w1_e': nrm((DEPTH, N_EXPERTS, D_MODEL, D_EXPERT), D_MODEL ** -0.5),
        'w3_e': nrm((DEPTH, N_EXPERTS, D_MODEL, D_EXPERT), D_MODEL ** -0.5),
        'w2_e': nrm((DEPTH, N_EXPERTS, D_EXPERT, D_MODEL), D_EXPERT ** -0.5 * BETA),
        'w1_s': nrm((DEPTH, D_MODEL, D_EXPERT), D_MODEL ** -0.5),
        'w3_s': nrm((DEPTH, D_MODEL, D_EXPERT), D_MODEL ** -0.5),
        'w2_s': nrm((DEPTH, D_EXPERT, D_MODEL), D_EXPERT ** -0.5 * BETA),
        'w_ple_gate': nrm((DEPTH, D_MODEL, D_MODEL), D_MODEL ** -0.5),
        'w_ple_proj': nrm((DEPTH, PLE_DIM, D_MODEL), PLE_DIM ** -0.5),
    }


def reference(x_prompt, x_sample, cache_a_k, cache_a_v, state_b_ssm, state_b_conv, cache_c_k, cache_c_v,
              p_prompt, p_sample, w_in_even, sink_a, conv_w_b, conv_b_b, dt_bias_b, a_log_b, d_skip_b,
              norm_w_b, w_out_even, w_in_odd, w_out_odd, ln_g, ln_b, router_w, router_bias,
              w1_e, w3_e, w2_e, w1_s, w3_s, w2_s, w_ple_gate, w_ple_proj):
    xp, xs = x_prompt, x_sample
    nbp = xp.shape[0]
    ssm0_p = jnp.zeros((nbp, H_B, HD_B, D_STATE), state_b_ssm.dtype)
    conv0_p = jnp.zeros((nbp, CONV_K - 1, CONV_DIM), xp.dtype)
    pa_k, pa_v, pb_ssm, pb_conv, pc_k, pc_v = [], [], [], [], [], []
    sa_k, sa_v, sb_ssm, sb_conv, sc_k, sc_v = [], [], [], [], [], []
    for i in range(DEPTH):
        j = i // 2
        if i % 2 == 0:
            ev = (w_in_even[j], sink_a[j], conv_w_b[j], conv_b_b[j], dt_bias_b[j], a_log_b[j],
                  d_skip_b[j], norm_w_b[j], w_out_even[j])
            hp, (k_, v_, s_, c_) = _even_mixer(xp, 0, None, None, ssm0_p, conv0_p, *ev)
            pa_k.append(k_); pa_v.append(v_); pb_ssm.append(s_); pb_conv.append(c_)
            hs, (k_, v_, s_, c_) = _even_mixer(xs, PAST_LEN, cache_a_k[j], cache_a_v[j],
                                               state_b_ssm[j], state_b_conv[j], *ev)
            sa_k.append(k_); sa_v.append(v_); sb_ssm.append(s_); sb_conv.append(c_)
        else:
            hp, (k_, v_) = _odd_mixer(xp, 0, None, None, w_in_odd[j], w_out_odd[j])
            pc_k.append(k_); pc_v.append(v_)
            hs, (k_, v_) = _odd_mixer(xs, PAST_LEN, cache_c_k[j], cache_c_v[j], w_in_odd[j], w_out_odd[j])
            sc_k.append(k_); sc_v.append(v_)
        lw = (ln_g[i], ln_b[i], router_w[i], router_bias[i], w1_e[i], w3_e[i], w2_e[i],
              w1_s[i], w3_s[i], w2_s[i], w_ple_gate[i], w_ple_proj[i])
        xp = _finish_layer(xp, hp, p_prompt[i], *lw)
        xs = _finish_layer(xs, hs, p_sample[i], *lw)
    prompt_a_k, prompt_a_v = jnp.stack(pa_k), jnp.stack(pa_v)
    prompt_b_ssm, prompt_b_conv = jnp.stack(pb_ssm), jnp.stack(pb_conv)
    prompt_c_k, prompt_c_v = jnp.stack(pc_k), jnp.stack(pc_v)
    sample_a_k, sample_a_v = jnp.stack(sa_k), jnp.stack(sa_v)
    sample_b_ssm, sample_b_conv = jnp.stack(sb_ssm), jnp.stack(sb_conv)
    sample_c_k, sample_c_v = jnp.stack(sc_k), jnp.stack(sc_v)
    return (xp, xs, prompt_a_k, prompt_a_v, prompt_b_ssm, prompt_b_conv, prompt_c_k, prompt_c_v,
            sample_a_k, sample_a_v, sample_b_ssm, sample_b_conv, sample_c_k, sample_c_v)
```

```python
import functools
import math

import numpy as np
import jax
import jax.numpy as jnp
from jax import lax
from jax.experimental import pallas as pl
from jax.experimental.pallas import tpu as pltpu

F32 = jnp.float32
BF16 = jnp.bfloat16
I32 = jnp.int32
U32 = jnp.uint32

D_MODEL = 2048
DEPTH = 4
PAST_LEN = 16384
ALPHA = (2.0 * DEPTH) ** 0.25
LN_EPS = 1e-5
RMS_EPS = 1e-5
ROPE_THETA = 10000.0
BLK = 128

HD_A = 64
H_A = D_MODEL // (2 * HD_A)
KV_A = H_A // 8
WINDOW_A = 128

HD_B = 64
D_INNER = D_MODEL // 2
H_B = D_INNER // HD_B
NG_B = 2
D_STATE = 128
CONV_K = 4
CONV_DIM = D_INNER + 2 * NG_B * D_STATE
EVEN_IN = H_A * HD_A + 2 * KV_A * HD_A + D_INNER + CONV_DIM + H_B
OFF_K_A = H_A * HD_A
OFF_V_A = OFF_K_A + KV_A * HD_A
OFF_Z = OFF_V_A + KV_A * HD_A
OFF_XBC = OFF_Z + D_INNER
OFF_DT = OFF_XBC + CONV_DIM

HD_C = 128
H_C = D_MODEL // HD_C
KV_C = 4
DILATIONS = ((128, 1), (512, 4), (2048, 16))
N_DIL = len(DILATIONS)
QW_C = H_C * HD_C
KVW_C = KV_C * HD_C
ODD_IN = N_DIL * QW_C + 2 * KVW_C
OFF_K_C = N_DIL * QW_C
OFF_V_C = OFF_K_C + KVW_C

N_EXPERTS = 64
N_EXPERT_GROUPS = 8
TOPK_GROUPS = 4
TOP_K = 8
D_EXPERT = D_MODEL // 4
ROUTED_SCALE = 2.5
PLE_DIM = 256

SAMPLE_PAD = 8
LANES = 128
VMEM_LIMIT = 56 * 1024 * 1024
EXPERT_TILE = 256
NEG = -1e30


def _cparams(sem, vmem=None):
    return pltpu.CompilerParams(dimension_semantics=sem, vmem_limit_bytes=vmem or VMEM_LIMIT)


def _pick(n, cands):
    for c in cands:
        if n % c == 0:
            return c
    raise ValueError(f"no tile for {n}")


def _mm_kernel(x_ref, w_ref, o_ref, xb_ref):
    @pl.when(pl.program_id(1) == 0)
    def _():
        xb_ref[...] = x_ref[...].astype(BF16)

    o_ref[...] = jnp.dot(xb_ref[...], w_ref[...].astype(BF16), preferred_element_type=F32)


def _matmul(x, w, tn=512):
    M, K = x.shape
    N = w.shape[1]
    tm = _pick(M, (768, 512, 384, 256, 128))
    return pl.pallas_call(
        _mm_kernel,
        grid=(M // tm, pl.cdiv(N, tn)),
        in_specs=[pl.BlockSpec((tm, K), lambda i, j: (i, 0)),
                  pl.BlockSpec((K, tn), lambda i, j: (0, j))],
        out_specs=pl.BlockSpec((tm, tn), lambda i, j: (i, j)),
        out_shape=jax.ShapeDtypeStruct((M, N), F32),
        scratch_shapes=[pltpu.VMEM((tm, K), BF16)],
        compiler_params=_cparams(("parallel", "arbitrary")),
        name="matmul",
    )(x, w)


def _layernorm_rows(v, g, b):
    mu = jnp.mean(v, axis=-1, keepdims=True)
    vc = v - mu
    var = jnp.mean(vc * vc, axis=-1, keepdims=True)
    return vc * lax.rsqrt(var + LN_EPS) * g + b


def _add_ln_kernel(x_ref, h_ref, g_ref, b_ref, o_ref):
    o_ref[...] = _layernorm_rows(ALPHA * x_ref[...] + h_ref[...], g_ref[...], b_ref[...])


def _add_ln(x, h, g, b):
    T, D = x.shape
    tm = _pick(T, (256, 128))
    row = pl.BlockSpec((tm, D), lambda i: (i, 0))
    vec = pl.BlockSpec((1, D), lambda i: (0, 0))
    return pl.pallas_call(
        _add_ln_kernel, grid=(T // tm,), in_specs=[row, row, vec, vec], out_specs=row,
        out_shape=jax.ShapeDtypeStruct((T, D), F32),
        compiler_params=_cparams(("parallel",)), name="add_ln",
    )(x, h, g.reshape(1, D), b.reshape(1, D))


def _rope_tables(pos, head_dim, scale=1.0):
    half = head_dim // 2
    inv = ROPE_THETA ** (-np.arange(half, dtype=np.float64) / half)
    ang = np.asarray(pos, np.float64)[:, None] * inv
    cos, sin = np.cos(ang), np.sin(ang)
    reps = LANES // head_dim
    cos_t = np.tile(np.concatenate([cos, cos], -1), (1, reps)) * scale
    sin_t = np.tile(np.concatenate([-sin, sin], -1), (1, reps)) * scale
    return jnp.asarray(cos_t, F32), jnp.asarray(sin_t, F32)


def _rope_chunk(x, cos, sin, head_dim):
    half = head_dim // 2
    if head_dim == LANES:
        rot = pltpu.roll(x, half, axis=1)
    else:
        lane = lax.broadcasted_iota(I32, x.shape, 1)
        first = (lane & (head_dim - 1)) < half
        rot = jnp.where(first, pltpu.roll(x, LANES - half, axis=1), pltpu.roll(x, half, axis=1))
    return x * cos + rot * sin


def _rope_kv_kernel(k_ref, v_ref, cos_ref, sin_ref, o_ref, *, head_dim, width):
    cos, sin = cos_ref[...], sin_ref[...]
    for c in range(width // LANES):
        sl = slice(c * LANES, (c + 1) * LANES)
        o_ref[:, sl] = _rope_chunk(k_ref[:, sl], cos, sin, head_dim)
    o_ref[:, width:] = v_ref[...]


def _rope_kv(proj, cos_all, sin_all, n_prompt_blocks, pos_blocks, head_dim, width, k_off):
    T = proj.shape[0]
    kb = k_off // width
    tab = lambda i: (jnp.where(i < n_prompt_blocks, i % pos_blocks, pos_blocks), 0)
    return pl.pallas_call(
        functools.partial(_rope_kv_kernel, head_dim=head_dim, width=width),
        grid=(T // BLK,),
        in_specs=[pl.BlockSpec((BLK, width), lambda i: (i, kb)),
                  pl.BlockSpec((BLK, width), lambda i: (i, kb + 1)),
                  pl.BlockSpec((BLK, LANES), tab), pl.BlockSpec((BLK, LANES), tab)],
        out_specs=pl.BlockSpec((BLK, 2 * width), lambda i: (i, 0)),
        out_shape=jax.ShapeDtypeStruct((T, 2 * width), F32),
        compiler_params=_cparams(("parallel",)), name="rope_kv",
    )(proj, proj, cos_all, sin_all)


def _softmax_pv(s, vv, sink_col=None):
    m = jnp.max(s, axis=1, keepdims=True)
    if sink_col is not None:
        m = jnp.maximum(m, sink_col)
    e = jnp.exp(s - m)
    den = jnp.sum(e, axis=1, keepdims=True)
    if sink_col is not None:
        den = den + jnp.exp(sink_col - m)
    o = jnp.dot(e.astype(BF16), vv, preferred_element_type=F32) / den
    return o, m + jnp.log(den)


def _qk(qs, kk):
    return lax.dot_general(qs, kk, (((1,), (1,)), ((), ())), preferred_element_type=F32)


def _band_mask(rows, first_block):
    qi = lax.broadcasted_iota(I32, (rows, 2 * BLK), 0) & (BLK - 1)
    kj = lax.broadcasted_iota(I32, (rows, 2 * BLK), 1)
    ok = (kj >= qi) & (kj <= qi + BLK)
    return ok & ((kj >= BLK) | jnp.logical_not(first_block))


def _dup_half(x, which):
    lane = lax.broadcasted_iota(I32, x.shape, 1)
    sw = pltpu.roll(x, HD_A, axis=1)
    return jnp.where(lane < HD_A, x, sw) if which == 0 else jnp.where(lane < HD_A, sw, x)


def _attn_a_heads(q_chunks, kdup, vdup, mask, sink_ref, kvh):
    R = q_chunks[0].shape[0]
    lane = lax.broadcasted_iota(I32, (R, LANES), 1)
    lo = lane < HD_A
    parts, sinks = [], []
    for c, ch in enumerate(q_chunks):
        parts.append(jnp.where(lo, ch, 0.0))
        parts.append(jnp.where(lo, 0.0, ch))
        for par in range(2):
            sinks.append(jnp.full((R, 1), sink_ref[kvh * 8 + 2 * c + par], F32))
    qs = jnp.concatenate(parts, axis=0).astype(BF16)
    s = jnp.where(mask, _qk(qs, kdup), NEG)
    o, _ = _softmax_pv(s, vdup, jnp.concatenate(sinks, axis=0))
    return [jnp.where(lo, o[(2 * c) * R:(2 * c + 1) * R], o[(2 * c + 1) * R:(2 * c + 2) * R])
            for c in range(len(q_chunks))]


def _attn_a_prompt_kernel(sink_ref, q_ref, kvp_ref, kvc_ref, cos_ref, sin_ref, o_ref):
    first = pl.program_id(1) == 0
    cos, sin = cos_ref[...], sin_ref[...]
    kvp, kvc = kvp_ref[...], kvc_ref[...]
    kk = jnp.concatenate([kvp[:, :LANES], kvc[:, :LANES]], axis=0)
    vv = jnp.concatenate([kvp[:, LANES:], kvc[:, LANES:]], axis=0)
    mask = _band_mask(8 * BLK, first)
    for kvh in range(KV_A):
        kdup = _dup_half(kk, kvh).astype(BF16)
        vdup = _dup_half(vv, kvh).astype(BF16)
        chunks = [_rope_chunk(q_ref[:, (kvh * 4 + c) * LANES:(kvh * 4 + c + 1) * LANES], cos, sin, HD_A)
                  for c in range(4)]
        outs = _attn_a_heads(chunks, kdup, vdup, mask, sink_ref, kvh)
        for c in range(4):
            o_ref[:, (kvh * 4 + c) * LANES:(kvh * 4 + c + 1) * LANES] = outs[c]


def _attn_a_prompt(proj, kv, sink, cosq, sinq, B, L):
    T = proj.shape[0]
    nb = L // BLK
    return pl.pallas_call(
        _attn_a_prompt_kernel,
        grid=(B, nb),
        in_specs=[pl.BlockSpec(memory_space=pltpu.SMEM),
                  pl.BlockSpec((BLK, OFF_K_A), lambda b, m: (b * nb + m, 0)),
                  pl.BlockSpec((BLK, 2 * LANES), lambda b, m: (b * nb + jnp.maximum(m - 1, 0), 0)),
                  pl.BlockSpec((BLK, 2 * LANES), lambda b, m: (b * nb + m, 0)),
                  pl.BlockSpec((BLK, LANES), lambda b, m: (m, 0)),
                  pl.BlockSpec((BLK, LANES), lambda b, m: (m, 0))],
        out_specs=pl.BlockSpec((BLK, OFF_K_A), lambda b, m: (b * nb + m, 0)),
        out_shape=jax.ShapeDtypeStruct((T, D_MODEL), F32),
        compiler_params=_cparams(("parallel", "parallel")), name="attn_a_prompt",
    )(sink, proj, kv, kv, cosq, sinq)


def _attn_a_sample_kernel(sink_ref, q_ref, kvn_ref, ck_ref, cv_ref, cos_ref, sin_ref, mix_ref, o_ref):
    del mix_ref
    R = SAMPLE_PAD
    cos, sin = cos_ref[...], sin_ref[...]
    kvn = kvn_ref[...]
    zpad = jnp.zeros((BLK - R, LANES), F32)
    kk = jnp.concatenate([ck_ref[...], kvn[:, :LANES], zpad], axis=0)
    vv = jnp.concatenate([cv_ref[...], kvn[:, LANES:], zpad], axis=0)
    t = lax.broadcasted_iota(I32, (8 * R, 2 * BLK), 0) & (R - 1)
    j = lax.broadcasted_iota(I32, (8 * R, 2 * BLK), 1)
    mask = ((j < BLK) & (j >= t)) | ((j >= BLK) & (j - BLK <= t) & (j < BLK + R))
    for kvh in range(KV_A):
        kdup = _dup_half(kk, kvh).astype(BF16)
        vdup = _dup_half(vv, kvh).astype(BF16)
        chunks = [_rope_chunk(q_ref[:, (kvh * 4 + c) * LANES:(kvh * 4 + c + 1) * LANES], cos, sin, HD_A)
                  for c in range(4)]
        outs = _attn_a_heads(chunks, kdup, vdup, mask, sink_ref, kvh)
        for c in range(4):
            o_ref[:, (kvh * 4 + c) * LANES:(kvh * 4 + c + 1) * LANES] = outs[c]


def _attn_a_sample(mix, proj, kv, cache_k, cache_v, sink, cosq, sinq, n_prompt, DB):
    r0 = n_prompt // SAMPLE_PAD
    lb = cache_k.shape[1]
    assert lb == BLK
    return pl.pallas_call(
        _attn_a_sample_kernel,
        grid=(DB,),
        in_specs=[pl.BlockSpec(memory_space=pltpu.SMEM),
                  pl.BlockSpec((SAMPLE_PAD, OFF_K_A), lambda b: (r0 + b, 0)),
                  pl.BlockSpec((SAMPLE_PAD, 2 * LANES), lambda b: (r0 + b, 0)),
                  pl.BlockSpec((None, lb, LANES), lambda b: (b, 0, 0)),
                  pl.BlockSpec((None, lb, LANES), lambda b: (b, 0, 0)),
                  pl.BlockSpec((SAMPLE_PAD, LANES), lambda b: (0, 0)),
                  pl.BlockSpec((SAMPLE_PAD, LANES), lambda b: (0, 0)),
                  pl.BlockSpec(memory_space=pl.ANY)],
        out_specs=pl.BlockSpec((SAMPLE_PAD, OFF_K_A), lambda b: (r0 + b, 0)),
        out_shape=jax.ShapeDtypeStruct(mix.shape, F32),
        input_output_aliases={7: 0},
        compiler_params=_cparams(("parallel",)), name="attn_a_sample",
    )(sink, proj, kv, cache_k.reshape(DB, lb, LANES), cache_v.reshape(DB, lb, LANES), cosq, sinq, mix)


def _hi_dot(a, b):
    return jnp.dot(a, b, precision=lax.Precision.HIGHEST, preferred_element_type=F32)


def _silu(x):
    return x * jax.nn.sigmoid(x)


def _ssd_kernel(proj_ref, carry0_ref, h0_ref, cw_ref, cb_ref, dtb_ref, alog_ref, dskip_ref, nw_ref,
                exp_ref, mix_ref, y_ref, hout_ref, carry_ref, state_ref, *, rows, valid, nc):
    del mix_ref
    c = pl.program_id(1)
    Q = BLK

    @pl.when(c == 0)
    def _():
        carry_ref[...] = carry0_ref[...]
        state_ref[...] = h0_ref[...].T

    blk = proj_ref[...]
    if rows < Q:
        blk = jnp.concatenate([blk, jnp.zeros((Q - rows, blk.shape[1]), F32)], axis=0)
    z = blk[:, OFF_Z:OFF_XBC]
    xbc = blk[:, OFF_XBC:OFF_DT]
    dt_raw = blk[:, OFF_DT:OFF_DT + H_B]

    ext = jnp.concatenate([carry_ref[...], xbc], axis=0)
    u = cb_ref[...] + xbc * cw_ref[CONV_K - 1:CONV_K, :]
    for s in range(1, CONV_K):
        u = u + pltpu.roll(ext, s, axis=0)[8:] * cw_ref[CONV_K - 1 - s:CONV_K - s, :]
    carry_ref[...] = xbc[Q - 8:]
    u = _silu(u)
    xs = u[:, :D_INNER]
    bmat = u[:, D_INNER:D_INNER + NG_B * D_STATE]
    cmat = u[:, D_INNER + NG_B * D_STATE:]

    row_q = lax.broadcasted_iota(I32, (Q, LANES), 0)
    lane = lax.broadcasted_iota(I32, (Q, LANES), 1)
    dt_pad = jnp.concatenate([dt_raw, jnp.zeros((Q, LANES - H_B), F32)], axis=1)
    dtv = jnp.where((lane < H_B) & (row_q < valid), jax.nn.softplus(dt_pad + dtb_ref[...]), 0.0)
    adt = dtv * (-jnp.exp(alog_ref[...]))
    li = lax.broadcasted_iota(I32, (Q, Q), 0)
    si = lax.broadcasted_iota(I32, (Q, Q), 1)
    causal = si <= li
    acum = _hi_dot(causal.astype(F32), adt)
    acum_t = acum.T
    a_last = acum[Q - 1:Q, :]

    expand = exp_ref[...]
    xdt = xs * _hi_dot(dtv, expand)
    in_decay = _hi_dot(jnp.exp(acum), expand)
    to_end = _hi_dot(jnp.exp(a_last - acum), expand)
    chunk_decay = in_decay[Q - 1:Q, :]

    state = state_ref[...]
    state_b = state.astype(BF16)
    lo = lane < HD_B
    y_groups, new_state = [], []
    hpg = H_B // NG_B
    for g in range(NG_B):
        bg = bmat[:, g * D_STATE:(g + 1) * D_STATE]
        cg = cmat[:, g * D_STATE:(g + 1) * D_STATE].astype(BF16)
        cbm = _qk(cg, bg.astype(BF16))
        ch0, ch1 = g * hpg * HD_B, (g + 1) * hpg * HD_B
        pairs = []
        for pr in range(hpg // 2):
            col = ch0 + pr * LANES
            xpair = xdt[:, col:col + LANES].astype(BF16)
            ys = []
            for par in range(2):
                h = g * hpg + 2 * pr + par
                seg = acum[:, h:h + 1] - acum_t[h:h + 1, :]
                dec = jnp.exp(jnp.where(causal, seg, NEG))
                ys.append(jnp.dot((cbm * dec).astype(BF16), xpair, preferred_element_type=F32))
            pairs.append(jnp.where(lo, ys[0], ys[1]))
        y_off = jnp.dot(cg, state_b[:, ch0:ch1], preferred_element_type=F32)
        y_groups.append(jnp.concatenate(pairs, axis=1) + y_off * in_decay[:, ch0:ch1])
        xe = (xdt[:, ch0:ch1] * to_end[:, ch0:ch1]).astype(BF16)
        new_state.append(jnp.dot(bg.T.astype(BF16), xe, preferred_element_type=F32))
    y = jnp.concatenate(y_groups, axis=1) + xs * dskip_ref[...]
    state_ref[...] = state * chunk_decay + jnp.concatenate(new_state, axis=1)

    ug = y * _silu(z)
    gw = D_INNER // NG_B
    outs = []
    for g in range(NG_B):
        v = ug[:, g * gw:(g + 1) * gw]
        outs.append(v * lax.rsqrt(jnp.mean(v * v, axis=1, keepdims=True) + RMS_EPS))
    yn = jnp.concatenate(outs, axis=1) * nw_ref[...]
    y_ref[...] = yn[:rows]

    @pl.when(c == nc - 1)
    def _():
        hout_ref[...] = state_ref[...].T


def _ssd(mix, proj, carry0, h0, conv_w, conv_b, dt_bias, a_log, d_skip, norm_w, *, row0, nbatch, nc, rows, valid):
    expand = jnp.asarray(np.concatenate([np.repeat(np.eye(H_B, dtype=np.float32), HD_B, axis=1),
                                         np.zeros((LANES - H_B, D_INNER), np.float32)], axis=0))
    pad_h = lambda v: jnp.concatenate([v, jnp.zeros((LANES - H_B,), F32)]).reshape(1, LANES)
    cw = jnp.concatenate([conv_w, jnp.zeros((8 - CONV_K, CONV_DIM), F32)], axis=0)
    rb0 = row0 // rows
    vec = lambda n: pl.BlockSpec((1, n), lambda b, c: (0, 0))
    y_new, h_new = pl.pallas_call(
        functools.partial(_ssd_kernel, rows=rows, valid=valid, nc=nc),
        grid=(nbatch, nc),
        in_specs=[pl.BlockSpec((rows, EVEN_IN), lambda b, c: (rb0 + b * nc + c, 0)),
                  pl.BlockSpec((None, 8, CONV_DIM), lambda b, c: (b, 0, 0)),
                  pl.BlockSpec((None, D_INNER, D_STATE), lambda b, c: (b, 0, 0)),
                  pl.BlockSpec((8, CONV_DIM), lambda b, c: (0, 0)),
                  vec(CONV_DIM), vec(LANES), vec(LANES), vec(D_INNER), vec(D_INNER),
                  pl.BlockSpec((LANES, D_INNER), lambda b, c: (0, 0)),
                  pl.BlockSpec(memory_space=pl.ANY)],
        out_specs=[pl.BlockSpec((rows, D_INNER), lambda b, c: (rb0 + b * nc + c, 1)),
                   pl.BlockSpec((None, D_INNER, D_STATE), lambda b, c: (b, 0, 0))],
        out_shape=[jax.ShapeDtypeStruct(mix.shape, F32),
                   jax.ShapeDtypeStruct((nbatch, D_INNER, D_STATE), F32)],
        scratch_shapes=[pltpu.VMEM((8, CONV_DIM), F32), pltpu.VMEM((D_STATE, D_INNER), F32)],
        input_output_aliases={10: 0},
        compiler_params=_cparams(("parallel", "arbitrary")), name="ssd",
    )(proj, carry0, h0.reshape(nbatch, D_INNER, D_STATE), cw, conv_b.reshape(1, -1), pad_h(dt_bias),
      pad_h(a_log), jnp.repeat(d_skip, HD_B).reshape(1, -1), norm_w.reshape(1, -1), expand, mix)
    return y_new, h_new.reshape(nbatch, H_B, HD_B, D_STATE)


def _attn_c_prompt_kernel(q0_ref, q1_ref, q2_ref, q3_ref, kvp_ref, kvc_ref, cos_ref, sin_ref, o_ref, lse_ref):
    first = pl.program_id(1) == 0
    cos, sin = cos_ref[...], sin_ref[...]
    kvp, kvc = kvp_ref[...], kvc_ref[...]
    G = H_C // KV_C
    mask = _band_mask(G * BLK, first)
    lane = lax.broadcasted_iota(I32, (BLK, LANES), 1)
    lse_tile = jnp.zeros((BLK, LANES), F32)
    for kvh, q_ref in enumerate((q0_ref, q1_ref, q2_ref, q3_ref)):
        ks = slice(kvh * HD_C, (kvh + 1) * HD_C)
        vs = slice(KVW_C + kvh * HD_C, KVW_C + (kvh + 1) * HD_C)
        kk = jnp.concatenate([kvp[:, ks], kvc[:, ks]], axis=0).astype(BF16)
        vv = jnp.concatenate([kvp[:, vs], kvc[:, vs]], axis=0).astype(BF16)
        qs = jnp.concatenate([_rope_chunk(q_ref[:, i * HD_C:(i + 1) * HD_C], cos, sin, HD_C) for i in range(G)],
                             axis=0).astype(BF16)
        o, lse = _softmax_pv(jnp.where(mask, _qk(qs, kk), NEG), vv)
        for i in range(G):
            h = kvh * G + i
            o_ref[:, h * HD_C:(h + 1) * HD_C] = o[i * BLK:(i + 1) * BLK]
            lse_tile = jnp.where(lane == h, lse[i * BLK:(i + 1) * BLK], lse_tile)
    lse_ref[...] = lse_tile


def _attn_c_prompt(proj, kv, cosq, sinq, g, dil, B, L):
    T = proj.shape[0]
    n = L // dil
    nb = n // BLK
    Tv = T // dil
    qcb = ODD_IN // KVW_C
    G = H_C // KV_C
    qspec = lambda kvh: pl.BlockSpec(
        (BLK, G * HD_C), lambda bi, m: ((bi // dil) * nb + m, (bi % dil) * qcb + g * KV_C + kvh))
    return pl.pallas_call(
        _attn_c_prompt_kernel,
        grid=(B * dil, nb),
        in_specs=[qspec(0), qspec(1), qspec(2), qspec(3),
                  pl.BlockSpec((BLK, 2 * KVW_C), lambda bi, m: ((bi // dil) * nb + jnp.maximum(m - 1, 0), bi % dil)),
                  pl.BlockSpec((BLK, 2 * KVW_C), lambda bi, m: ((bi // dil) * nb + m, bi % dil)),
                  pl.BlockSpec((BLK, LANES), lambda bi, m: (m, bi % dil)),
                  pl.BlockSpec((BLK, LANES), lambda bi, m: (m, bi % dil))],
        out_specs=[pl.BlockSpec((BLK, QW_C), lambda bi, m: ((bi // dil) * nb + m, bi % dil)),
                   pl.BlockSpec((BLK, LANES), lambda bi, m: ((bi // dil) * nb + m, bi % dil))],
        out_shape=[jax.ShapeDtypeStruct((Tv, dil * QW_C), F32),
                   jax.ShapeDtypeStruct((Tv, dil * LANES), F32)],
        compiler_params=_cparams(("parallel", "parallel")), name=f"attn_c_prompt_d{dil}",
    )(*([proj.reshape(Tv, dil * ODD_IN)] * 4), kv.reshape(Tv, dil * 2 * KVW_C), kv.reshape(Tv, dil * 2 * KVW_C),
      cosq.reshape(L // dil, dil * LANES), sinq.reshape(L // dil, dil * LANES))


def _mix_c_kernel(o0_ref, o1_ref, o2_ref, l0_ref, l1_ref, l2_ref, o_ref):
    ls = [l0_ref[...], l1_ref[...], l2_ref[...]]
    m = jnp.maximum(jnp.maximum(ls[0], ls[1]), ls[2])
    es = [jnp.exp(l - m) for l in ls]
    inv = 1.0 / (es[0] + es[1] + es[2])
    ws = [e * inv for e in es]
    for h in range(H_C):
        sl = slice(h * HD_C, (h + 1) * HD_C)
        o_ref[:, sl] = (ws[0][:, h:h + 1] * o0_ref[:, sl] + ws[1][:, h:h + 1] * o1_ref[:, sl]
                        + ws[2][:, h:h + 1] * o2_ref[:, sl])


def _mix_c(outs, lses, n_prompt):
    T = outs[0].shape[0]
    tm = BLK
    ospec = pl.BlockSpec((tm, QW_C), lambda i: (i, 0))
    lspec = pl.BlockSpec((tm, LANES), lambda i: (i, 0))
    return pl.pallas_call(
        _mix_c_kernel, grid=(n_prompt // tm,),
        in_specs=[ospec] * 3 + [lspec] * 3, out_specs=ospec,
        out_shape=jax.ShapeDtypeStruct((T, QW_C), F32),
        compiler_params=_cparams(("parallel",)), name="mix_c",
    )(*outs, *lses)


def _attn_c_sample_kernel(q_ref, kvn_ref, ck_ref, cv_ref, cos_ref, sin_ref, mix_ref, o_ref, *, lb):
    del mix_ref
    R = SAMPLE_PAD
    G = H_C // KV_C
    cos, sin = cos_ref[...], sin_ref[...]
    kvn = kvn_ref[...]
    zpad = jnp.zeros((BLK - R, HD_C), F32)
    nrow = N_DIL * G * R
    rho = lax.broadcasted_iota(I32, (nrow, 1), 0)
    t = rho & (R - 1)
    grp = rho // (G * R)
    dil_m1 = jnp.where(grp == 0, DILATIONS[0][1] - 1, jnp.where(grp == 1, DILATIONS[1][1] - 1, DILATIONS[2][1] - 1))
    win = jnp.where(grp == 0, DILATIONS[0][0], jnp.where(grp == 1, DILATIONS[1][0], DILATIONS[2][0]))
    jc = lax.broadcasted_iota(I32, (nrow, lb), 1)
    dc = lb + t - jc
    mask_c = ((dc & dil_m1) == 0) & (dc <= win)
    jn = lax.broadcasted_iota(I32, (nrow, BLK), 1)
    dn = t - jn
    mask_n = (dn >= 0) & ((dn & dil_m1) == 0) & (jn < R)
    for kvh in range(KV_C):
        ks = slice(kvh * HD_C, (kvh + 1) * HD_C)
        vs = slice(KVW_C + kvh * HD_C, KVW_C + (kvh + 1) * HD_C)
        kc = ck_ref[:, ks].astype(BF16)
        vc = cv_ref[:, ks].astype(BF16)
        kn = jnp.concatenate([kvn[:, ks], zpad], axis=0).astype(BF16)
        vn = jnp.concatenate([kvn[:, vs], zpad], axis=0).astype(BF16)
        parts = []
        for g in range(N_DIL):
            for i in range(G):
                c0 = g * QW_C + (kvh * G + i) * HD_C
                parts.append(_rope_chunk(q_ref[:, c0:c0 + HD_C], cos, sin, HD_C))
        qs = jnp.concatenate(parts, axis=0).astype(BF16)
        sc = jnp.where(mask_c, _qk(qs, kc), NEG)
        sn = jnp.where(mask_n, _qk(qs, kn), NEG)
        m = jnp.maximum(jnp.max(sc, axis=1, keepdims=True), jnp.max(sn, axis=1, keepdims=True))
        ec, en = jnp.exp(sc - m), jnp.exp(sn - m)
        den = jnp.sum(ec, axis=1, keepdims=True) + jnp.sum(en, axis=1, keepdims=True)
        o = (jnp.dot(ec.astype(BF16), vc, preferred_element_type=F32)
             + jnp.dot(en.astype(BF16), vn, preferred_element_type=F32)) / den
        lse = m + jnp.log(den)
        gr = G * R
        lg = [lse[g * gr:(g + 1) * gr] for g in range(N_DIL)]
        mm = jnp.maximum(jnp.maximum(lg[0], lg[1]), lg[2])
        eg = [jnp.exp(l - mm) for l in lg]
        inv = 1.0 / (eg[0] + eg[1] + eg[2])
        om = sum((eg[g] * inv) * o[g * gr:(g + 1) * gr] for g in range(N_DIL))
        for i in range(G):
            h = kvh * G + i
            o_ref[:, h * HD_C:(h + 1) * HD_C] = om[i * R:(i + 1) * R]


def _attn_c_sample(mix, proj, kv, cache_k, cache_v, cosq, sinq, n_prompt, DB):
    r0 = n_prompt // SAMPLE_PAD
    lb = cache_k.shape[1]
    return pl.pallas_call(
        functools.partial(_attn_c_sample_kernel, lb=lb),
        grid=(DB,),
        in_specs=[pl.BlockSpec((SAMPLE_PAD, ODD_IN), lambda b: (r0 + b, 0)),
                  pl.BlockSpec((SAMPLE_PAD, 2 * KVW_C), lambda b: (r0 + b, 0)),
                  pl.BlockSpec((None, lb, KVW_C), lambda b: (b, 0, 0)),
                  pl.BlockSpec((None, lb, KVW_C), lambda b: (b, 0, 0)),
                  pl.BlockSpec((SAMPLE_PAD, LANES), lambda b: (0, 0)),
                  pl.BlockSpec((SAMPLE_PAD, LANES), lambda b: (0, 0)),
                  pl.BlockSpec(memory_space=pl.ANY)],
        out_specs=pl.BlockSpec((SAMPLE_PAD, QW_C), lambda b: (r0 + b, 0)),
        out_shape=jax.ShapeDtypeStruct(mix.shape, F32),
        input_output_aliases={6: 0},
        compiler_params=_cparams(("parallel",)), name="attn_c_sample",
    )(proj, kv, cache_k.reshape(DB, lb, KVW_C), cache_v.reshape(DB, lb, KVW_C), cosq, sinq, mix)


def _row_max(v):
    return jnp.max(v, axis=1, keepdims=True)


def _first_argmax(v, m, lane):
    return jnp.min(jnp.where(v == m, lane, LANES), axis=1, keepdims=True)


def _router_kernel(x_ref, rw_ref, rb_ref, idx_ref, gate_ref, rank_ref, cnt_ref, carry_ref):
    tm = x_ref.shape[0]

    @pl.when(pl.program_id(0) == 0)
    def _():
        carry_ref[...] = jnp.zeros_like(carry_ref)

    ninf = -jnp.inf
    scores = jax.nn.sigmoid(_hi_dot(x_ref[...], rw_ref[...]))
    lane = lax.broadcasted_iota(I32, (tm, LANES), 1)
    valid = lane < N_EXPERTS
    choice = jnp.where(valid, scores + rb_ref[...], ninf)
    per_group = N_EXPERTS // N_EXPERT_GROUPS
    grp = lane // per_group
    gs = jnp.full((tm, LANES), ninf, F32)
    for g in range(N_EXPERT_GROUPS):
        vg = jnp.where(grp == g, choice, ninf)
        m1 = _row_max(vg)
        i1 = _first_argmax(vg, m1, lane)
        m2 = _row_max(jnp.where(lane == i1, ninf, vg))
        gs = jnp.where(lane == g, m1 + m2, gs)
    emask = jnp.zeros((tm, LANES), jnp.bool_)
    for _ in range(TOPK_GROUPS):
        m = _row_max(gs)
        gi = _first_argmax(gs, m, lane)
        emask = emask | (grp == gi)
        gs = jnp.where(lane == gi, ninf, gs)
    sel = jnp.where(emask & valid, choice, ninf)
    hots, idxs, gates = [], [], []
    for _ in range(TOP_K):
        m = _row_max(sel)
        ik = _first_argmax(sel, m, lane)
        hot = lane == ik
        hots.append(hot)
        idxs.append(ik)
        gates.append(jnp.sum(jnp.where(hot, scores, 0.0), axis=1, keepdims=True))
        sel = jnp.where(hot, ninf, sel)
    gsum = sum(gates)
    onehot = sum(h.astype(F32) for h in hots)
    ri = lax.broadcasted_iota(I32, (tm, tm), 0)
    ci = lax.broadcasted_iota(I32, (tm, tm), 1)
    before = jnp.dot((ci < ri).astype(BF16), onehot.astype(BF16), preferred_element_type=F32)
    base = before + carry_ref[0:1, :]
    lane8 = lax.broadcasted_iota(I32, (tm, TOP_K), 1)
    idx_o = jnp.zeros((tm, TOP_K), I32)
    gate_o = jnp.zeros((tm, TOP_K), F32)
    rank_o = jnp.zeros((tm, TOP_K), I32)
    for k in range(TOP_K):
        rk = jnp.sum(jnp.where(hots[k], base, 0.0), axis=1, keepdims=True).astype(I32)
        idx_o = jnp.where(lane8 == k, idxs[k], idx_o)
        gate_o = jnp.where(lane8 == k, gates[k] / gsum * ROUTED_SCALE, gate_o)
        rank_o = jnp.where(lane8 == k, rk, rank_o)
    idx_ref[...] = idx_o
    gate_ref[...] = gate_o
    rank_ref[...] = rank_o
    total = carry_ref[...] + jnp.sum(onehot, axis=0, keepdims=True)
    carry_ref[...] = total
    cnt_ref[...] = total


def _router(x, router_w, router_bias):
    T, D = x.shape
    tm = _pick(T, (256, 128))
    rw = jnp.concatenate([router_w, jnp.zeros((D, LANES - N_EXPERTS), F32)], axis=1)
    rb = jnp.concatenate([router_bias, jnp.zeros((LANES - N_EXPERTS,), F32)]).reshape(1, LANES)
    small = lambda: pl.BlockSpec((tm, TOP_K), lambda i: (i, 0))
    return pl.pallas_call(
        _router_kernel, grid=(T // tm,),
        in_specs=[pl.BlockSpec((tm, D), lambda i: (i, 0)),
                  pl.BlockSpec((D, LANES), lambda i: (0, 0)),
                  pl.BlockSpec((1, LANES), lambda i: (0, 0))],
        out_specs=[small(), small(), small(), pl.BlockSpec((8, LANES), lambda i: (0, 0))],
        out_shape=[jax.ShapeDtypeStruct((T, TOP_K), I32), jax.ShapeDtypeStruct((T, TOP_K), F32),
                   jax.ShapeDtypeStruct((T, TOP_K), I32), jax.ShapeDtypeStruct((8, LANES), F32)],
        scratch_shapes=[pltpu.VMEM((8, LANES), F32)],
        compiler_params=_cparams(("arbitrary",)), name="router",
    )(x, rw, rb)


def _pack_bf16_pairs(x):
    half = x.shape[1] // 2
    lo = pltpu.bitcast(x[:, :half].astype(BF16).astype(F32), U32) >> 16
    hi = pltpu.bitcast(x[:, half:].astype(BF16).astype(F32), U32) & jnp.uint32(0xFFFF0000)
    return hi | lo


def _unpack_bf16_pairs(w):
    lo = pltpu.bitcast(w << 16, F32).astype(BF16)
    hi = pltpu.bitcast(w & jnp.uint32(0xFFFF0000), F32).astype(BF16)
    return lo, hi


def _dispatch_kernel(zstart_ref, zcount_ref, x_ref, pos_hbm, xs_hbm, packed_ref, zrow_ref, pos_ref, psem, sem):
    i = pl.program_id(0)
    tm = x_ref.shape[0]
    pcopy = pltpu.make_async_copy(pos_hbm.at[i], pos_ref, psem)
    pcopy.start()
    packed_ref[...] = _pack_bf16_pairs(x_ref[...])

    def row_copy(src, r, dst_row):
        return pltpu.make_async_copy(src.at[pl.ds(r, 1)], xs_hbm.at[pl.ds(dst_row, 1)], sem)

    @pl.when(i == 0)
    def _():
        zrow_ref[...] = jnp.zeros_like(zrow_ref)

        def per_expert(e, carry):
            def start(r, c):
                row_copy(zrow_ref, 0, zstart_ref[e] + r).start()
                return c

            def wait(r, c):
                row_copy(zrow_ref, 0, 0).wait()
                return c

            lax.fori_loop(0, zcount_ref[e], start, 0)
            lax.fori_loop(0, zcount_ref[e], wait, 0)
            return carry

        lax.fori_loop(0, N_EXPERTS, per_expert, 0)

    pcopy.wait()

    def start(r, c):
        for k in range(TOP_K):
            row_copy(packed_ref, r, pos_ref[r * TOP_K + k]).start()
        return c

    def wait(r, c):
        for k in range(TOP_K):
            row_copy(packed_ref, 0, 0).wait()
        return c

    lax.fori_loop(0, tm, start, 0)
    lax.fori_loop(0, tm, wait, 0)


def _dispatch(x, pos, zstart, zcount, n_slots):
    T, D = x.shape
    tm = _pick(T, (256, 128))
    return pl.pallas_call(
        _dispatch_kernel,
        grid_spec=pltpu.PrefetchScalarGridSpec(
            num_scalar_prefetch=2, grid=(T // tm,),
            in_specs=[pl.BlockSpec((tm, D), lambda i, *_: (i, 0)),
                      pl.BlockSpec(memory_space=pl.ANY)],
            out_specs=pl.BlockSpec(memory_space=pl.ANY),
            scratch_shapes=[pltpu.VMEM((tm, D // 2), U32), pltpu.VMEM((8, D // 2), U32),
                            pltpu.SMEM((tm * TOP_K,), I32),
                            pltpu.SemaphoreType.DMA, pltpu.SemaphoreType.DMA]),
        out_shape=jax.ShapeDtypeStruct((n_slots, D // 2), U32),
        compiler_params=pltpu.CompilerParams(dimension_semantics=("arbitrary",), vmem_limit_bytes=VMEM_LIMIT,
                                             has_side_effects=True),
        name="moe_dispatch",
    )(zstart, zcount, x, pos.reshape(T // tm, tm * TOP_K))


def _experts_kernel(te_ref, nu_ref, xs_ref, w1_ref, w3_ref, w2_ref, ys_ref, w1b, w3b, w2b):
    i = pl.program_id(0)

    @pl.when(i < nu_ref[0])
    def _():
        prev = te_ref[jnp.maximum(i - 1, 0)]

        @pl.when((i == 0) | (te_ref[i] != prev))
        def _():
            w1b[...] = w1_ref[...].astype(BF16)
            w3b[...] = w3_ref[...].astype(BF16)
            w2b[...] = w2_ref[...].astype(BF16)

        lo, hi = _unpack_bf16_pairs(xs_ref[...])
        half = lo.shape[1]
        h1 = (jnp.dot(lo, w1b[:half, :], preferred_element_type=F32)
              + jnp.dot(hi, w1b[half:, :], preferred_element_type=F32))
        h3 = (jnp.dot(lo, w3b[:half, :], preferred_element_type=F32)
              + jnp.dot(hi, w3b[half:, :], preferred_element_type=F32))
        ys_ref[...] = jnp.dot((_silu(h1) * h3).astype(BF16), w2b[...], preferred_element_type=F32)


def _experts(xs, tile_e, n_used, w1, w3, w2):
    n_slots = xs.shape[0]
    nt = n_slots // EXPERT_TILE
    D, DE = w1.shape[1], w1.shape[2]
    live = lambda i, te, nu: jnp.minimum(i, nu[0] - 1)
    return pl.pallas_call(
        _experts_kernel,
        grid_spec=pltpu.PrefetchScalarGridSpec(
            num_scalar_prefetch=2, grid=(nt,),
            in_specs=[pl.BlockSpec((EXPERT_TILE, D // 2), lambda i, te, nu: (live(i, te, nu), 0)),
                      pl.BlockSpec((None, D, DE), lambda i, te, nu: (te[live(i, te, nu)], 0, 0)),
                      pl.BlockSpec((None, D, DE), lambda i, te, nu: (te[live(i, te, nu)], 0, 0)),
                      pl.BlockSpec((None, DE, D), lambda i, te, nu: (te[live(i, te, nu)], 0, 0))],
            out_specs=pl.BlockSpec((EXPERT_TILE, D), lambda i, te, nu: (live(i, te, nu), 0)),
            scratch_shapes=[pltpu.VMEM((D, DE), BF16), pltpu.VMEM((D, DE), BF16), pltpu.VMEM((DE, D), BF16)]),
        out_shape=jax.ShapeDtypeStruct((n_slots, D), F32),
        compiler_params=_cparams(("arbitrary",)), name="moe_experts",
    )(tile_e, n_used, xs, w1, w3, w2)


def _shared_kernel(x_ref, w1_ref, w3_ref, w2_ref, o_ref, w1b, w3b, w2b):
    @pl.when(pl.program_id(0) == 0)
    def _():
        w1b[...] = w1_ref[...].astype(BF16)
        w3b[...] = w3_ref[...].astype(BF16)
        w2b[...] = w2_ref[...].astype(BF16)

    xb = x_ref[...].astype(BF16)
    h1 = jnp.dot(xb, w1b[...], preferred_element_type=F32)
    h3 = jnp.dot(xb, w3b[...], preferred_element_type=F32)
    o_ref[...] = jnp.dot((_silu(h1) * h3).astype(BF16), w2b[...], preferred_element_type=F32)


def _shared_expert(x, w1, w3, w2):
    T, D = x.shape
    DE = w1.shape[1]
    tm = _pick(T, (256, 128))
    const = lambda r, c: pl.BlockSpec((r, c), lambda i: (0, 0))
    return pl.pallas_call(
        _shared_kernel, grid=(T // tm,),
        in_specs=[pl.BlockSpec((tm, D), lambda i: (i, 0)), const(D, DE), const(D, DE), const(DE, D)],
        out_specs=pl.BlockSpec((tm, D), lambda i: (i, 0)),
        out_shape=jax.ShapeDtypeStruct((T, D), F32),
        scratch_shapes=[pltpu.VMEM((D, DE), BF16), pltpu.VMEM((D, DE), BF16), pltpu.VMEM((DE, D), BF16)],
        compiler_params=_cparams(("arbitrary",)), name="moe_shared",
    )(x, w1, w3, w2)


def _combine_kernel(x_ref, sh_ref, gate_ref, g_ref, b_ref, pos_hbm, ys_hbm, o_ref, buf_ref, pos_ref, psem, sem):
    i = pl.program_id(0)
    tm = x_ref.shape[0]
    pcopy = pltpu.make_async_copy(pos_hbm.at[i], pos_ref, psem)
    pcopy.start()
    pcopy.wait()

    def row_copy(src_row, k, r):
        return pltpu.make_async_copy(ys_hbm.at[pl.ds(src_row, 1)], buf_ref.at[k, pl.ds(r, 1)], sem)

    def start(r, c):
        for k in range(TOP_K):
            row_copy(pos_ref[r * TOP_K + k], k, r).start()
        return c

    def wait(r, c):
        for k in range(TOP_K):
            row_copy(0, 0, 0).wait()
        return c

    lax.fori_loop(0, tm, start, 0)
    lax.fori_loop(0, tm, wait, 0)
    gate = gate_ref[...]
    y = sh_ref[...]
    for k in range(TOP_K):
        y = y + gate[:, k:k + 1] * buf_ref[k]
    o_ref[...] = _layernorm_rows(ALPHA * x_ref[...] + y, g_ref[...], b_ref[...])


def _combine(x, shared, gate, pos, ys, g, b):
    T, D = x.shape
    tm = BLK
    row = pl.BlockSpec((tm, D), lambda i: (i, 0))
    vec = pl.BlockSpec((1, D), lambda i: (0, 0))
    return pl.pallas_call(
        _combine_kernel, grid=(T // tm,),
        in_specs=[row, row, pl.BlockSpec((tm, TOP_K), lambda i: (i, 0)), vec, vec,
                  pl.BlockSpec(memory_space=pl.ANY), pl.BlockSpec(memory_space=pl.ANY)],
        out_specs=row,
        out_shape=jax.ShapeDtypeStruct((T, D), F32),
        scratch_shapes=[pltpu.VMEM((TOP_K, tm, D), F32), pltpu.SMEM((tm * TOP_K,), I32),
                        pltpu.SemaphoreType.DMA, pltpu.SemaphoreType.DMA],
        compiler_params=_cparams(("arbitrary",)), name="moe_combine",
    )(x, shared, gate, g.reshape(1, D), b.reshape(1, D), pos.reshape(T // tm, tm * TOP_K), ys)


def _moe_ln(x, g, b, router_w, router_bias, w1, w3, w2, w1s, w3s, w2s):
    T = x.shape[0]
    idx, gate, rank, cnt = _router(x, router_w, router_bias)
    te = EXPERT_TILE
    counts = cnt[0, :N_EXPERTS].astype(I32)
    padded = (counts + te - 1) // te * te
    pad_end = jnp.cumsum(padded)
    pad_start = pad_end - padded
    pos = pad_start[idx] + rank
    nt = (T * TOP_K + N_EXPERTS * (te - 1)) // te + 1
    n_used = (pad_end[-1] // te).astype(I32)
    tiles = jnp.minimum(jnp.arange(nt, dtype=I32), n_used - 1)
    tile_e = jnp.minimum(jnp.searchsorted(pad_end, tiles * te, side="right"), N_EXPERTS - 1).astype(I32)
    xs = _dispatch(x, pos, (pad_start + counts).astype(I32), (padded - counts).astype(I32), nt * te)
    ys = _experts(xs, tile_e, n_used.reshape(1), w1, w3, w2)
    shared = _shared_expert(x, w1s, w3s, w2s)
    return _combine(x, shared, gate, pos, ys, g, b)


def _ple_kernel(x_ref, xr_ref, p_ref, wg_ref, wp_ref, o_ref, xb_ref, pb_ref):
    @pl.when(pl.program_id(1) == 0)
    def _():
        xb_ref[...] = x_ref[...].astype(BF16)
        pb_ref[...] = p_ref[...].astype(BF16)

    gate = jax.nn.sigmoid(jnp.dot(xb_ref[...], wg_ref[...].astype(BF16), preferred_element_type=F32))
    proj = jnp.dot(pb_ref[...], wp_ref[...].astype(BF16), preferred_element_type=F32)
    o_ref[...] = xr_ref[...] + gate * proj


def _ple(x, p, wg, wp, tn=512):
    T, D = x.shape
    P = p.shape[1]
    tm = _pick(T, (768, 512, 384, 256, 128))
    return pl.pallas_call(
        _ple_kernel, grid=(T // tm, D // tn),
        in_specs=[pl.BlockSpec((tm, D), lambda i, j: (i, 0)),
                  pl.BlockSpec((tm, tn), lambda i, j: (i, j)),
                  pl.BlockSpec((tm, P), lambda i, j: (i, 0)),
                  pl.BlockSpec((D, tn), lambda i, j: (0, j)),
                  pl.BlockSpec((P, tn), lambda i, j: (0, j))],
        out_specs=pl.BlockSpec((tm, tn), lambda i, j: (i, j)),
        out_shape=jax.ShapeDtypeStruct((T, D), F32),
        scratch_shapes=[pltpu.VMEM((tm, D), BF16), pltpu.VMEM((tm, P), BF16)],
        compiler_params=_cparams(("parallel", "arbitrary")), name="ple",
    )(x, x, p, wg, wp)


def _pad_time(a, axis=1):
    pad = [(0, 0)] * a.ndim
    pad[axis] = (0, SAMPLE_PAD - a.shape[axis])
    return jnp.pad(a, pad)


def kernel(x_prompt, x_sample, cache_a_k, cache_a_v, state_b_ssm, state_b_conv, cache_c_k, cache_c_v, p_prompt, p_sample, w_in_even, sink_a, conv_w_b, conv_b_b, dt_bias_b, a_log_b, d_skip_b, norm_w_b, w_out_even, w_in_odd, w_out_odd, ln_g, ln_b, router_w, router_bias, w1_e, w3_e, w2_e, w1_s, w3_s, w2_s, w_ple_gate, w_ple_proj):
    B, L, D = x_prompt.shape
    DB, TS, _ = x_sample.shape
    NP = B * L
    assert L % (BLK * max(d for _, d in DILATIONS)) == 0 and TS <= SAMPLE_PAD and CONV_K - 1 <= TS
    NS = DB * SAMPLE_PAD
    npb = NP // BLK

    x = jnp.concatenate([x_prompt.reshape(NP, D), _pad_time(x_sample).reshape(NS, D)], axis=0)

    prompt_pos = np.arange(L)
    sample_pos = PAST_LEN + np.arange(SAMPLE_PAD)
    all_pos = np.concatenate([prompt_pos, np.tile(sample_pos, BLK // SAMPLE_PAD)])
    tabs = {}
    for name, hd in (("a", HD_A), ("c", HD_C)):
        sc = hd ** -0.5
        tabs[name] = dict(k=_rope_tables(all_pos, hd), qp=_rope_tables(prompt_pos, hd, sc),
                          qs=_rope_tables(sample_pos, hd, sc))

    zeros_carry = jnp.zeros((B, 8, CONV_DIM), F32)
    zeros_state = jnp.zeros((B, H_B, HD_B, D_STATE), F32)

    outs = {k: [] for k in ("pa_k", "pa_v", "pb_ssm", "pb_conv", "pc_k", "pc_v",
                            "sa_k", "sa_v", "sb_ssm", "sb_conv", "sc_k", "sc_v")}
    for i in range(DEPTH):
        j = i // 2
        if i % 2 == 0:
            proj = _matmul(x, w_in_even[j])
            kv = _rope_kv(proj, *tabs["a"]["k"], npb, L // BLK, HD_A, KV_A * HD_A, OFF_K_A)
            mix = _attn_a_prompt(proj, kv, sink_a[j], *tabs["a"]["qp"], B, L)
            mix = _attn_a_sample(mix, proj, kv, cache_a_k[j], cache_a_v[j], sink_a[j], *tabs["a"]["qs"], NP, DB)
            ssm_w = (conv_w_b[j], conv_b_b[j], dt_bias_b[j], a_log_b[j], d_skip_b[j], norm_w_b[j])
            mix, hp = _ssd(mix, proj, zeros_carry, zeros_state, *ssm_w,
                           row0=0, nbatch=B, nc=L // BLK, rows=BLK, valid=BLK)
            carry_s = jnp.pad(state_b_conv[j], ((0, 0), (8 - (CONV_K - 1), 0), (0, 0)))
            mix, hs = _ssd(mix, proj, carry_s, state_b_ssm[j], *ssm_w,
                           row0=NP, nbatch=DB, nc=1, rows=SAMPLE_PAD, valid=TS)
            h = _matmul(mix, w_out_even[j])
            kvp = kv[:NP].reshape(B, L, 2, KV_A, HD_A)
            kvs = kv[NP:].reshape(DB, SAMPLE_PAD, 2, KV_A, HD_A)
            keep = min(WINDOW_A, L)
            outs["pa_k"].append(kvp[:, L - keep:, 0]); outs["pa_v"].append(kvp[:, L - keep:, 1])
            outs["sa_k"].append(kvs[:, :TS, 0]); outs["sa_v"].append(kvs[:, :TS, 1])
            outs["pb_ssm"].append(hp); outs["sb_ssm"].append(hs)
            outs["pb_conv"].append(proj[:NP].reshape(B, L, EVEN_IN)[:, L - (CONV_K - 1):, OFF_XBC:OFF_DT])
            outs["sb_conv"].append(proj[NP:].reshape(DB, SAMPLE_PAD, EVEN_IN)[:, TS - (CONV_K - 1):TS, OFF_XBC:OFF_DT])
        else:
            proj = _matmul(x, w_in_odd[j])
            kv = _rope_kv(proj, *tabs["c"]["k"], npb, L // BLK, HD_C, KVW_C, OFF_K_C)
            go, gl = [], []
            for g, (win, dil) in enumerate(DILATIONS):
                assert win // dil == BLK
                o_g, l_g = _attn_c_prompt(proj, kv, *tabs["c"]["qp"], g, dil, B, L)
                go.append(o_g.reshape(NP + NS, QW_C)); gl.append(l_g.reshape(NP + NS, LANES))
            mix = _mix_c(go, gl, NP)
            mix = _attn_c_sample(mix, proj, kv, cache_c_k[j], cache_c_v[j], *tabs["c"]["qs"], NP, DB)
            h = _matmul(mix, w_out_odd[j])
            kvp = kv[:NP].reshape(B, L, 2, KV_C, HD_C)
            kvs = kv[NP:].reshape(DB, SAMPLE_PAD, 2, KV_C, HD_C)
            outs["pc_k"].append(kvp[:, :, 0]); outs["pc_v"].append(kvp[:, :, 1])
            outs["sc_k"].append(kvs[:, :TS, 0]); outs["sc_v"].append(kvs[:, :TS, 1])
        x = _add_ln(x, h, ln_g[i, 0], ln_b[i, 0])
        x = _moe_ln(x, ln_g[i, 1], ln_b[i, 1], router_w[i], router_bias[i], w1_e[i], w3_e[i], w2_e[i],
                    w1_s[i], w3_s[i], w2_s[i])
        p = jnp.concatenate([p_prompt[i].reshape(NP, PLE_DIM), _pad_time(p_sample[i]).reshape(NS, PLE_DIM)], axis=0)
        x = _ple(x, p, w_ple_gate[i], w_ple_proj[i])

    st = lambda k: jnp.stack(outs[k])
    y_prompt = x[:NP].reshape(B, L, D)
    y_sample = x[NP:].reshape(DB, SAMPLE_PAD, D)[:, :TS]
    return (y_prompt, y_sample, st("pa_k"), st("pa_v"), st("pb_ssm"), st("pb_conv"), st("pc_k"), st("pc_v"),
            st("sa_k"), st("sa_v"), st("sb_ssm"), st("sb_conv"), st("sc_k"), st("sc_v"))
```

```python
import functools
import math

import numpy as np
import jax
import jax.numpy as jnp
from jax import lax
from jax.experimental import pallas as pl
from jax.experimental.pallas import tpu as pltpu

F32 = jnp.float32
BF16 = jnp.bfloat16
I32 = jnp.int32
U32 = jnp.uint32

D_MODEL = 2048
DEPTH = 4
PAST_LEN = 16384
ALPHA = (2.0 * DEPTH) ** 0.25
LN_EPS = 1e-5
RMS_EPS = 1e-5
ROPE_THETA = 10000.0
BLK = 128

HD_A = 64
H_A = D_MODEL // (2 * HD_A)
KV_A = H_A // 8
WINDOW_A = 128

HD_B = 64
D_INNER = D_MODEL // 2
H_B = D_INNER // HD_B
NG_B = 2
D_STATE = 128
CONV_K = 4
CONV_DIM = D_INNER + 2 * NG_B * D_STATE
EVEN_IN = H_A * HD_A + 2 * KV_A * HD_A + D_INNER + CONV_DIM + H_B
OFF_K_A = H_A * HD_A
OFF_V_A = OFF_K_A + KV_A * HD_A
OFF_Z = OFF_V_A + KV_A * HD_A
OFF_XBC = OFF_Z + D_INNER
OFF_DT = OFF_XBC + CONV_DIM

HD_C = 128
H_C = D_MODEL // HD_C
KV_C = 4
DILATIONS = ((128, 1), (512, 4), (2048, 16))
N_DIL = len(DILATIONS)
QW_C = H_C * HD_C
KVW_C = KV_C * HD_C
ODD_IN = N_DIL * QW_C + 2 * KVW_C
OFF_K_C = N_DIL * QW_C
OFF_V_C = OFF_K_C + KVW_C

N_EXPERTS = 64
N_EXPERT_GROUPS = 8
TOPK_GROUPS = 4
TOP_K = 8
D_EXPERT = D_MODEL // 4
ROUTED_SCALE = 2.5
PLE_DIM = 256

SAMPLE_PAD = 8
LANES = 128
VMEM_LIMIT = 56 * 1024 * 1024
EXPERT_TILE = 256
NEG = -1e30


def _cparams(sem, vmem=None):
    return pltpu.CompilerParams(dimension_semantics=sem, vmem_limit_bytes=vmem or VMEM_LIMIT)


def _pick(n, cands):
    for c in cands:
        if n % c == 0:
            return c
    raise ValueError(f"no tile for {n}")


def _mm_kernel(x_ref, w_ref, o_ref, xb_ref):
    @pl.when(pl.program_id(1) == 0)
    def _():
        xb_ref[...] = x_ref[...].astype(BF16)

    o_ref[...] = jnp.dot(xb_ref[...], w_ref[...].astype(BF16), preferred_element_type=F32)


def _matmul(x, w, layer, tn=512):
    M, K = x.shape
    N = w.shape[2]
    tm = _pick(M, (768, 512, 384, 256, 128))
    return pl.pallas_call(
        _mm_kernel,
        grid=(M // tm, pl.cdiv(N, tn)),
        in_specs=[pl.BlockSpec((tm, K), lambda i, j: (i, 0)),
                  pl.BlockSpec((None, K, tn), lambda i, j: (layer, 0, j))],
        out_specs=pl.BlockSpec((tm, tn), lambda i, j: (i, j)),
        out_shape=jax.ShapeDtypeStruct((M, N), F32),
        scratch_shapes=[pltpu.VMEM((tm, K), BF16)],
        compiler_params=_cparams(("parallel", "arbitrary")),
        name="matmul",
    )(x, w)


def _layernorm_rows(v, g, b):
    mu = jnp.mean(v, axis=-1, keepdims=True)
    vc = v - mu
    var = jnp.mean(vc * vc, axis=-1, keepdims=True)
    return vc * lax.rsqrt(var + LN_EPS) * g + b


def _add_ln_kernel(x_ref, h_ref, g_ref, b_ref, o_ref):
    o_ref[...] = _layernorm_rows(ALPHA * x_ref[...] + h_ref[...], g_ref[...], b_ref[...])


def _add_ln(x, h, g, b):
    T, D = x.shape
    tm = _pick(T, (256, 128))
    row = pl.BlockSpec((tm, D), lambda i: (i, 0))
    vec = pl.BlockSpec((1, D), lambda i: (0, 0))
    return pl.pallas_call(
        _add_ln_kernel, grid=(T // tm,), in_specs=[row, row, vec, vec], out_specs=row,
        out_shape=jax.ShapeDtypeStruct((T, D), F32),
        compiler_params=_cparams(("parallel",)), name="add_ln",
    )(x, h, g.reshape(1, D), b.reshape(1, D))


def _rope_tables(pos, head_dim, scale=1.0):
    half = head_dim // 2
    inv = ROPE_THETA ** (-np.arange(half, dtype=np.float64) / half)
    ang = np.asarray(pos, np.float64)[:, None] * inv
    cos, sin = np.cos(ang), np.sin(ang)
    reps = LANES // head_dim
    cos_t = np.tile(np.concatenate([cos, cos], -1), (1, reps)) * scale
    sin_t = np.tile(np.concatenate([-sin, sin], -1), (1, reps)) * scale
    return jnp.asarray(cos_t, F32), jnp.asarray(sin_t, F32)


def _rope_chunk(x, cos, sin, head_dim):
    half = head_dim // 2
    if head_dim == LANES:
        rot = pltpu.roll(x, half, axis=1)
    else:
        lane = lax.broadcasted_iota(I32, x.shape, 1)
        first = (lane & (head_dim - 1)) < half
        rot = jnp.where(first, pltpu.roll(x, LANES - half, axis=1), pltpu.roll(x, half, axis=1))
    return x * cos + rot * sin


def _rope_kv_kernel(k_ref, v_ref, cos_ref, sin_ref, o_ref, *head_refs, head_dim, width):
    cos, sin = cos_ref[...], sin_ref[...]
    for c in range(width // LANES):
        sl = slice(c * LANES, (c + 1) * LANES)
        kr = _rope_chunk(k_ref[:, sl], cos, sin, head_dim)
        o_ref[:, sl] = kr
        if head_refs:
            head_refs[0][:, c, :] = kr
            head_refs[1][:, c, :] = v_ref[:, sl]
    o_ref[:, width:] = v_ref[...]


def _rope_kv(proj, cos_all, sin_all, n_prompt_blocks, pos_blocks, head_dim, width, k_off, per_head=False):
    T = proj.shape[0]
    kb = k_off // width
    tab = lambda i: (jnp.where(i < n_prompt_blocks, i % pos_blocks, pos_blocks), 0)
    out_specs = [pl.BlockSpec((BLK, 2 * width), lambda i: (i, 0))]
    out_shape = [jax.ShapeDtypeStruct((T, 2 * width), F32)]
    if per_head:
        assert head_dim == LANES
        nh = width // LANES
        out_specs += [pl.BlockSpec((BLK, nh, LANES), lambda i: (i, 0, 0))] * 2
        out_shape += [jax.ShapeDtypeStruct((T, nh, LANES), F32)] * 2
    res = pl.pallas_call(
        functools.partial(_rope_kv_kernel, head_dim=head_dim, width=width),
        grid=(T // BLK,),
        in_specs=[pl.BlockSpec((BLK, width), lambda i: (i, kb)),
                  pl.BlockSpec((BLK, width), lambda i: (i, kb + 1)),
                  pl.BlockSpec((BLK, LANES), tab), pl.BlockSpec((BLK, LANES), tab)],
        out_specs=out_specs, out_shape=out_shape,
        compiler_params=_cparams(("parallel",)), name="rope_kv",
    )(proj, proj, cos_all, sin_all)
    return res if per_head else res[0]


def _softmax_pv(s, vv, sink_col=None):
    m = jnp.max(s, axis=1, keepdims=True)
    if sink_col is not None:
        m = jnp.maximum(m, sink_col)
    e = jnp.exp(s - m)
    den = jnp.sum(e, axis=1, keepdims=True)
    if sink_col is not None:
        den = den + jnp.exp(sink_col - m)
    o = jnp.dot(e.astype(BF16), vv, preferred_element_type=F32) / den
    return o, m + jnp.log(den)


def _qk(qs, kk):
    return lax.dot_general(qs, kk, (((1,), (1,)), ((), ())), preferred_element_type=F32)


def _band_mask(rows, first_block):
    qi = lax.broadcasted_iota(I32, (rows, 2 * BLK), 0) & (BLK - 1)
    kj = lax.broadcasted_iota(I32, (rows, 2 * BLK), 1)
    ok = (kj >= qi) & (kj <= qi + BLK)
    return ok & ((kj >= BLK) | jnp.logical_not(first_block))


def _dup_half(x, which):
    lane = lax.broadcasted_iota(I32, x.shape, 1)
    sw = pltpu.roll(x, HD_A, axis=1)
    return jnp.where(lane < HD_A, x, sw) if which == 0 else jnp.where(lane < HD_A, sw, x)


def _attn_a_heads(q_chunks, kdup, vdup, mask, sink_ref, kvh):
    R = q_chunks[0].shape[0]
    lane = lax.broadcasted_iota(I32, (R, LANES), 1)
    lo = lane < HD_A
    parts, sinks = [], []
    for c, ch in enumerate(q_chunks):
        parts.append(jnp.where(lo, ch, 0.0))
        parts.append(jnp.where(lo, 0.0, ch))
        for par in range(2):
            sinks.append(jnp.full((R, 1), sink_ref[kvh * 8 + 2 * c + par], F32))
    qs = jnp.concatenate(parts, axis=0).astype(BF16)
    s = jnp.where(mask, _qk(qs, kdup), NEG)
    o, _ = _softmax_pv(s, vdup, jnp.concatenate(sinks, axis=0))
    return [jnp.where(lo, o[(2 * c) * R:(2 * c + 1) * R], o[(2 * c + 1) * R:(2 * c + 2) * R])
            for c in range(len(q_chunks))]


def _attn_a_prompt_kernel(sink_ref, q_ref, kvp_ref, kvc_ref, cos_ref, sin_ref, o_ref):
    first = pl.program_id(1) == 0
    cos, sin = cos_ref[...], sin_ref[...]
    kvp, kvc = kvp_ref[...], kvc_ref[...]
    kk = jnp.concatenate([kvp[:, :LANES], kvc[:, :LANES]], axis=0)
    vv = jnp.concatenate([kvp[:, LANES:], kvc[:, LANES:]], axis=0)
    mask = _band_mask(8 * BLK, first)
    for kvh in range(KV_A):
        kdup = _dup_half(kk, kvh).astype(BF16)
        vdup = _dup_half(vv, kvh).astype(BF16)
        chunks = [_rope_chunk(q_ref[:, (kvh * 4 + c) * LANES:(kvh * 4 + c + 1) * LANES], cos, sin, HD_A)
                  for c in range(4)]
        outs = _attn_a_heads(chunks, kdup, vdup, mask, sink_ref, kvh)
        for c in range(4):
            o_ref[:, (kvh * 4 + c) * LANES:(kvh * 4 + c + 1) * LANES] = outs[c]


def _attn_a_prompt(proj, kv, sink, cosq, sinq, B, L):
    T = proj.shape[0]
    nb = L // BLK
    return pl.pallas_call(
        _attn_a_prompt_kernel,
        grid=(B, nb),
        in_specs=[pl.BlockSpec(memory_space=pltpu.SMEM),
                  pl.BlockSpec((BLK, OFF_K_A), lambda b, m: (b * nb + m, 0)),
                  pl.BlockSpec((BLK, 2 * LANES), lambda b, m: (b * nb + jnp.maximum(m - 1, 0), 0)),
                  pl.BlockSpec((BLK, 2 * LANES), lambda b, m: (b * nb + m, 0)),
                  pl.BlockSpec((BLK, LANES), lambda b, m: (m, 0)),
                  pl.BlockSpec((BLK, LANES), lambda b, m: (m, 0))],
        out_specs=pl.BlockSpec((BLK, OFF_K_A), lambda b, m: (b * nb + m, 0)),
        out_shape=jax.ShapeDtypeStruct((T, D_MODEL), F32),
        compiler_params=_cparams(("parallel", "parallel")), name="attn_a_prompt",
    )(sink, proj, kv, kv, cosq, sinq)


def _attn_a_sample_kernel(sink_ref, q_ref, kvn_ref, ck_ref, cv_ref, cos_ref, sin_ref, mix_ref, o_ref):
    del mix_ref
    R = SAMPLE_PAD
    cos, sin = cos_ref[...], sin_ref[...]
    kvn = kvn_ref[...]
    zpad = jnp.zeros((BLK - R, LANES), F32)
    kk = jnp.concatenate([ck_ref[...], kvn[:, :LANES], zpad], axis=0)
    vv = jnp.concatenate([cv_ref[...], kvn[:, LANES:], zpad], axis=0)
    t = lax.broadcasted_iota(I32, (8 * R, 2 * BLK), 0) & (R - 1)
    j = lax.broadcasted_iota(I32, (8 * R, 2 * BLK), 1)
    mask = ((j < BLK) & (j >= t)) | ((j >= BLK) & (j - BLK <= t) & (j < BLK + R))
    for kvh in range(KV_A):
        kdup = _dup_half(kk, kvh).astype(BF16)
        vdup = _dup_half(vv, kvh).astype(BF16)
        chunks = [_rope_chunk(q_ref[:, (kvh * 4 + c) * LANES:(kvh * 4 + c + 1) * LANES], cos, sin, HD_A)
                  for c in range(4)]
        outs = _attn_a_heads(chunks, kdup, vdup, mask, sink_ref, kvh)
        for c in range(4):
            o_ref[:, (kvh * 4 + c) * LANES:(kvh * 4 + c + 1) * LANES] = outs[c]


def _attn_a_sample(mix, proj, kv, cache_k, cache_v, sink, cosq, sinq, n_prompt, DB):
    r0 = n_prompt // SAMPLE_PAD
    lb = cache_k.shape[1]
    assert lb == BLK
    return pl.pallas_call(
        _attn_a_sample_kernel,
        grid=(DB,),
        in_specs=[pl.BlockSpec(memory_space=pltpu.SMEM),
                  pl.BlockSpec((SAMPLE_PAD, OFF_K_A), lambda b: (r0 + b, 0)),
                  pl.BlockSpec((SAMPLE_PAD, 2 * LANES), lambda b: (r0 + b, 0)),
                  pl.BlockSpec((None, lb, LANES), lambda b: (b, 0, 0)),
                  pl.BlockSpec((None, lb, LANES), lambda b: (b, 0, 0)),
                  pl.BlockSpec((SAMPLE_PAD, LANES), lambda b: (0, 0)),
                  pl.BlockSpec((SAMPLE_PAD, LANES), lambda b: (0, 0)),
                  pl.BlockSpec(memory_space=pl.ANY)],
        out_specs=pl.BlockSpec((SAMPLE_PAD, OFF_K_A), lambda b: (r0 + b, 0)),
        out_shape=jax.ShapeDtypeStruct(mix.shape, F32),
        input_output_aliases={7: 0},
        compiler_params=_cparams(("parallel",)), name="attn_a_sample",
    )(sink, proj, kv, cache_k.reshape(DB, lb, LANES), cache_v.reshape(DB, lb, LANES), cosq, sinq, mix)


def _hi_dot(a, b):
    return jnp.dot(a, b, precision=lax.Precision.HIGHEST, preferred_element_type=F32)


def _silu(x):
    return x * jax.nn.sigmoid(x)


def _ssd_kernel(proj_ref, carry0_ref, h0_ref, cw_ref, cb_ref, dtb_ref, alog_ref, dskip_ref, nw_ref,
                exp_ref, mix_ref, y_ref, hout_ref, carry_ref, state_ref, *, rows, valid, nc):
    del mix_ref
    c = pl.program_id(1)
    Q = BLK

    @pl.when(c == 0)
    def _():
        carry_ref[...] = carry0_ref[...]
        state_ref[...] = h0_ref[...].T

    blk = proj_ref[...]
    if rows < Q:
        blk = jnp.concatenate([blk, jnp.zeros((Q - rows, blk.shape[1]), F32)], axis=0)
    z = blk[:, OFF_Z:OFF_XBC]
    xbc = blk[:, OFF_XBC:OFF_DT]
    dt_raw = blk[:, OFF_DT:OFF_DT + H_B]

    ext = jnp.concatenate([carry_ref[...], xbc], axis=0)
    u = cb_ref[...] + xbc * cw_ref[CONV_K - 1:CONV_K, :]
    for s in range(1, CONV_K):
        u = u + pltpu.roll(ext, s, axis=0)[8:] * cw_ref[CONV_K - 1 - s:CONV_K - s, :]
    carry_ref[...] = xbc[Q - 8:]
    u = _silu(u)
    xs = u[:, :D_INNER]
    bmat = u[:, D_INNER:D_INNER + NG_B * D_STATE]
    cmat = u[:, D_INNER + NG_B * D_STATE:]

    row_q = lax.broadcasted_iota(I32, (Q, LANES), 0)
    lane = lax.broadcasted_iota(I32, (Q, LANES), 1)
    dt_pad = jnp.concatenate([dt_raw, jnp.zeros((Q, LANES - H_B), F32)], axis=1)
    dtv = jnp.where((lane < H_B) & (row_q < valid), jax.nn.softplus(dt_pad + dtb_ref[...]), 0.0)
    adt = dtv * (-jnp.exp(alog_ref[...]))
    li = lax.broadcasted_iota(I32, (Q, Q), 0)
    si = lax.broadcasted_iota(I32, (Q, Q), 1)
    causal = si <= li
    acum = _hi_dot(causal.astype(F32), adt)
    acum_t = acum.T
    a_last = acum[Q - 1:Q, :]

    expand = exp_ref[...]
    xdt = xs * _hi_dot(dtv, expand)
    in_decay = _hi_dot(jnp.exp(acum), expand)
    to_end = _hi_dot(jnp.exp(a_last - acum), expand)
    chunk_decay = in_decay[Q - 1:Q, :]

    state = state_ref[...]
    state_b = state.astype(BF16)
    lo = lane < HD_B
    y_groups, new_state = [], []
    hpg = H_B // NG_B
    for g in range(NG_B):
        bg = bmat[:, g * D_STATE:(g + 1) * D_STATE]
        cg = cmat[:, g * D_STATE:(g + 1) * D_STATE].astype(BF16)
        cbm = _qk(cg, bg.astype(BF16))
        ch0, ch1 = g * hpg * HD_B, (g + 1) * hpg * HD_B
        pairs = []
        for pr in range(hpg // 2):
            col = ch0 + pr * LANES
            xpair = xdt[:, col:col + LANES].astype(BF16)
            ys = []
            for par in range(2):
                h = g * hpg + 2 * pr + par
                seg = acum[:, h:h + 1] - acum_t[h:h + 1, :]
                dec = jnp.exp(jnp.where(causal, seg, NEG))
                ys.append(jnp.dot((cbm * dec).astype(BF16), xpair, preferred_element_type=F32))
            pairs.append(jnp.where(lo, ys[0], ys[1]))
        y_off = jnp.dot(cg, state_b[:, ch0:ch1], preferred_element_type=F32)
        y_groups.append(jnp.concatenate(pairs, axis=1) + y_off * in_decay[:, ch0:ch1])
        xe = (xdt[:, ch0:ch1] * to_end[:, ch0:ch1]).astype(BF16)
        new_state.append(jnp.dot(bg.T.astype(BF16), xe, preferred_element_type=F32))
    y = jnp.concatenate(y_groups, axis=1) + xs * dskip_ref[...]
    state_ref[...] = state * chunk_decay + jnp.concatenate(new_state, axis=1)

    ug = y * _silu(z)
    gw = D_INNER // NG_B
    outs = []
    for g in range(NG_B):
        v = ug[:, g * gw:(g + 1) * gw]
        outs.append(v * lax.rsqrt(jnp.mean(v * v, axis=1, keepdims=True) + RMS_EPS))
    yn = jnp.concatenate(outs, axis=1) * nw_ref[...]
    y_ref[...] = yn[:rows]

    @pl.when(c == nc - 1)
    def _():
        hout_ref[...] = state_ref[...].T


def _ssd(mix, proj, carry0, h0, conv_w, conv_b, dt_bias, a_log, d_skip, norm_w, *, row0, nbatch, nc, rows, valid,
         h0_base=0):
    expand = jnp.asarray(np.concatenate([np.repeat(np.eye(H_B, dtype=np.float32), HD_B, axis=1),
                                         np.zeros((LANES - H_B, D_INNER), np.float32)], axis=0))
    pad_h = lambda v: jnp.concatenate([v, jnp.zeros((LANES - H_B,), F32)]).reshape(1, LANES)
    cw = jnp.concatenate([conv_w, jnp.zeros((8 - CONV_K, CONV_DIM), F32)], axis=0)
    rb0 = row0 // rows
    vec = lambda n: pl.BlockSpec((1, n), lambda b, c: (0, 0))
    y_new, h_new = pl.pallas_call(
        functools.partial(_ssd_kernel, rows=rows, valid=valid, nc=nc),
        grid=(nbatch, nc),
        in_specs=[pl.BlockSpec((rows, EVEN_IN), lambda b, c: (rb0 + b * nc + c, 0)),
                  pl.BlockSpec((None, 8, CONV_DIM), lambda b, c: (b, 0, 0)),
                  pl.BlockSpec((None, D_INNER, D_STATE), lambda b, c: (h0_base + b, 0, 0)),
                  pl.BlockSpec((8, CONV_DIM), lambda b, c: (0, 0)),
                  vec(CONV_DIM), vec(LANES), vec(LANES), vec(D_INNER), vec(D_INNER),
                  pl.BlockSpec((LANES, D_INNER), lambda b, c: (0, 0)),
                  pl.BlockSpec(memory_space=pl.ANY)],
        out_specs=[pl.BlockSpec((rows, D_INNER), lambda b, c: (rb0 + b * nc + c, 1)),
                   pl.BlockSpec((None, D_INNER, D_STATE), lambda b, c: (b, 0, 0))],
        out_shape=[jax.ShapeDtypeStruct(mix.shape, F32),
                   jax.ShapeDtypeStruct((nbatch, D_INNER, D_STATE), F32)],
        scratch_shapes=[pltpu.VMEM((8, CONV_DIM), F32), pltpu.VMEM((D_STATE, D_INNER), F32)],
        input_output_aliases={10: 0},
        compiler_params=_cparams(("parallel", "arbitrary")), name="ssd",
    )(proj, carry0, h0, cw, conv_b.reshape(1, -1), pad_h(dt_bias),
      pad_h(a_log), jnp.repeat(d_skip, HD_B).reshape(1, -1), norm_w.reshape(1, -1), expand, mix)
    return y_new, h_new.reshape(nbatch, H_B, HD_B, D_STATE)


def _attn_c_prompt_kernel(q0_ref, q1_ref, q2_ref, q3_ref, kvp_ref, kvc_ref, cos_ref, sin_ref, o_ref, lse_ref):
    first = pl.program_id(1) == 0
    cos, sin = cos_ref[...], sin_ref[...]
    kvp, kvc = kvp_ref[...], kvc_ref[...]
    G = H_C // KV_C
    mask = _band_mask(G * BLK, first)
    lane = lax.broadcasted_iota(I32, (BLK, LANES), 1)
    lse_tile = jnp.zeros((BLK, LANES), F32)
    for kvh, q_ref in enumerate((q0_ref, q1_ref, q2_ref, q3_ref)):
        ks = slice(kvh * HD_C, (kvh + 1) * HD_C)
        vs = slice(KVW_C + kvh * HD_C, KVW_C + (kvh + 1) * HD_C)
        kk = jnp.concatenate([kvp[:, ks], kvc[:, ks]], axis=0).astype(BF16)
        vv = jnp.concatenate([kvp[:, vs], kvc[:, vs]], axis=0).astype(BF16)
        qs = jnp.concatenate([_rope_chunk(q_ref[:, i * HD_C:(i + 1) * HD_C], cos, sin, HD_C) for i in range(G)],
                             axis=0).astype(BF16)
        o, lse = _softmax_pv(jnp.where(mask, _qk(qs, kk), NEG), vv)
        for i in range(G):
            h = kvh * G + i
            o_ref[:, h * HD_C:(h + 1) * HD_C] = o[i * BLK:(i + 1) * BLK]
            lse_tile = jnp.where(lane == h, lse[i * BLK:(i + 1) * BLK], lse_tile)
    lse_ref[...] = lse_tile


def _attn_c_prompt(proj, kv, cosq, sinq, g, B, L):
    T = proj.shape[0]
    nb = L // BLK
    G = H_C // KV_C
    qspec = lambda kvh: pl.BlockSpec((BLK, G * HD_C), lambda b, m: (b * nb + m, g * KV_C + kvh))
    return pl.pallas_call(
        _attn_c_prompt_kernel,
        grid=(B, nb),
        in_specs=[qspec(0), qspec(1), qspec(2), qspec(3),
                  pl.BlockSpec((BLK, 2 * KVW_C), lambda b, m: (b * nb + jnp.maximum(m - 1, 0), 0)),
                  pl.BlockSpec((BLK, 2 * KVW_C), lambda b, m: (b * nb + m, 0)),
                  pl.BlockSpec((BLK, LANES), lambda b, m: (m, 0)),
                  pl.BlockSpec((BLK, LANES), lambda b, m: (m, 0))],
        out_specs=[pl.BlockSpec((BLK, QW_C), lambda b, m: (b * nb + m, 0)),
                   pl.BlockSpec((BLK, LANES), lambda b, m: (b * nb + m, 0))],
        out_shape=[jax.ShapeDtypeStruct((T, QW_C), F32), jax.ShapeDtypeStruct((T, LANES), F32)],
        compiler_params=_cparams(("parallel", "parallel")), name="attn_c_prompt_d1",
    )(proj, proj, proj, proj, kv, kv, cosq, sinq)


def _attn_c_dilated_kernel(q0_ref, q1_ref, q2_ref, q3_ref, kp_ref, kc_ref, vp_ref, vc_ref, cos_ref, sin_ref,
                           o_ref, lse_ref, osc_ref, *, dil):
    first = pl.program_id(1) == 0
    kvh = pl.program_id(2)
    q_refs = (q0_ref, q1_ref, q2_ref, q3_ref)
    G = len(q_refs)
    mask = _band_mask(G * BLK, first)
    lane = lax.broadcasted_iota(I32, (BLK, LANES), 1)

    @pl.when(kvh == 0)
    def _():
        lse_ref[...] = jnp.zeros_like(lse_ref)

    for r in range(dil):
        rows = pl.ds(r, BLK, stride=dil)
        cos, sin = cos_ref[rows, :], sin_ref[rows, :]
        kk = jnp.concatenate([kp_ref[rows, :], kc_ref[rows, :]], axis=0).astype(BF16)
        vv = jnp.concatenate([vp_ref[rows, :], vc_ref[rows, :]], axis=0).astype(BF16)
        qs = jnp.concatenate([_rope_chunk(q_ref[rows, :], cos, sin, HD_C) for q_ref in q_refs], axis=0).astype(BF16)
        o, lse = _softmax_pv(jnp.where(mask, _qk(qs, kk), NEG), vv)
        lse_tile = lse_ref[rows, :]
        for i in range(G):
            osc_ref[i, rows, :] = o[i * BLK:(i + 1) * BLK]
            lse_tile = jnp.where(lane == kvh * G + i, lse[i * BLK:(i + 1) * BLK], lse_tile)
        lse_ref[rows, :] = lse_tile
    for i in range(G):
        o_ref[:, i * HD_C:(i + 1) * HD_C] = osc_ref[i]


def _attn_c_dilated(proj, kv, cosq, sinq, g, dil, B, L):
    T = proj.shape[0]
    R = BLK * dil
    nb = L // R
    G = H_C // KV_C
    vb = KVW_C // HD_C
    prev = lambda b, m: b * nb + jnp.maximum(m - 1, 0)
    qspec = lambda i: pl.BlockSpec((R, HD_C), lambda b, m, h: (b * nb + m, g * H_C + h * G + i))
    return pl.pallas_call(
        functools.partial(_attn_c_dilated_kernel, dil=dil),
        grid=(B, nb, KV_C),
        in_specs=[qspec(0), qspec(1), qspec(2), qspec(3),
                  pl.BlockSpec((R, HD_C), lambda b, m, h: (prev(b, m), h)),
                  pl.BlockSpec((R, HD_C), lambda b, m, h: (b * nb + m, h)),
                  pl.BlockSpec((R, HD_C), lambda b, m, h: (prev(b, m), vb + h)),
                  pl.BlockSpec((R, HD_C), lambda b, m, h: (b * nb + m, vb + h)),
                  pl.BlockSpec((R, LANES), lambda b, m, h: (m, 0)),
                  pl.BlockSpec((R, LANES), lambda b, m, h: (m, 0))],
        out_specs=[pl.BlockSpec((R, G * HD_C), lambda b, m, h: (b * nb + m, h)),
                   pl.BlockSpec((R, LANES), lambda b, m, h: (b * nb + m, 0))],
        out_shape=[jax.ShapeDtypeStruct((T, QW_C), F32), jax.ShapeDtypeStruct((T, LANES), F32)],
        scratch_shapes=[pltpu.VMEM((G, R, HD_C), F32)],
        compiler_params=_cparams(("parallel", "parallel", "arbitrary")), name=f"attn_c_prompt_d{dil}",
    )(proj, proj, proj, proj, kv, kv, kv, kv, cosq, sinq)


def _mix_c_kernel(o0_ref, o1_ref, o2_ref, l0_ref, l1_ref, l2_ref, o_ref):
    ls = [l0_ref[...], l1_ref[...], l2_ref[...]]
    m = jnp.maximum(jnp.maximum(ls[0], ls[1]), ls[2])
    es = [jnp.exp(l - m) for l in ls]
    inv = 1.0 / (es[0] + es[1] + es[2])
    ws = [e * inv for e in es]
    for h in range(H_C):
        sl = slice(h * HD_C, (h + 1) * HD_C)
        o_ref[:, sl] = (ws[0][:, h:h + 1] * o0_ref[:, sl] + ws[1][:, h:h + 1] * o1_ref[:, sl]
                        + ws[2][:, h:h + 1] * o2_ref[:, sl])


def _mix_c(outs, lses, n_prompt):
    T = outs[0].shape[0]
    tm = BLK
    ospec = pl.BlockSpec((tm, QW_C), lambda i: (i, 0))
    lspec = pl.BlockSpec((tm, LANES), lambda i: (i, 0))
    return pl.pallas_call(
        _mix_c_kernel, grid=(n_prompt // tm,),
        in_specs=[ospec] * 3 + [lspec] * 3, out_specs=ospec,
        out_shape=jax.ShapeDtypeStruct((T, QW_C), F32),
        compiler_params=_cparams(("parallel",)), name="mix_c",
    )(*outs, *lses)


def _attn_c_sample_kernel(q_ref, kvn_ref, ck_ref, cv_ref, cos_ref, sin_ref, mix_ref, o_ref, *, lb):
    del mix_ref
    R = SAMPLE_PAD
    G = H_C // KV_C
    cos, sin = cos_ref[...], sin_ref[...]
    kvn = kvn_ref[...]
    zpad = jnp.zeros((BLK - R, HD_C), F32)
    nrow = N_DIL * G * R
    rho = lax.broadcasted_iota(I32, (nrow, 1), 0)
    t = rho & (R - 1)
    grp = rho // (G * R)
    dil_m1 = jnp.where(grp == 0, DILATIONS[0][1] - 1, jnp.where(grp == 1, DILATIONS[1][1] - 1, DILATIONS[2][1] - 1))
    win = jnp.where(grp == 0, DILATIONS[0][0], jnp.where(grp == 1, DILATIONS[1][0], DILATIONS[2][0]))
    jc = lax.broadcasted_iota(I32, (nrow, lb), 1)
    dc = lb + t - jc
    mask_c = ((dc & dil_m1) == 0) & (dc <= win)
    jn = lax.broadcasted_iota(I32, (nrow, BLK), 1)
    dn = t - jn
    mask_n = (dn >= 0) & ((dn & dil_m1) == 0) & (jn < R)
    for kvh in range(KV_C):
        ks = slice(kvh * HD_C, (kvh + 1) * HD_C)
        vs = slice(KVW_C + kvh * HD_C, KVW_C + (kvh + 1) * HD_C)
        kc = ck_ref[:, kvh, :].astype(BF16)
        vc = cv_ref[:, kvh, :].astype(BF16)
        kn = jnp.concatenate([kvn[:, ks], zpad], axis=0).astype(BF16)
        vn = jnp.concatenate([kvn[:, vs], zpad], axis=0).astype(BF16)
        parts = []
        for g in range(N_DIL):
            for i in range(G):
                c0 = g * QW_C + (kvh * G + i) * HD_C
                parts.append(_rope_chunk(q_ref[:, c0:c0 + HD_C], cos, sin, HD_C))
        qs = jnp.concatenate(parts, axis=0).astype(BF16)
        sc = jnp.where(mask_c, _qk(qs, kc), NEG)
        sn = jnp.where(mask_n, _qk(qs, kn), NEG)
        m = jnp.maximum(jnp.max(sc, axis=1, keepdims=True), jnp.max(sn, axis=1, keepdims=True))
        ec, en = jnp.exp(sc - m), jnp.exp(sn - m)
        den = jnp.sum(ec, axis=1, keepdims=True) + jnp.sum(en, axis=1, keepdims=True)
        o = (jnp.dot(ec.astype(BF16), vc, preferred_element_type=F32)
             + jnp.dot(en.astype(BF16), vn, preferred_element_type=F32)) / den
        lse = m + jnp.log(den)
        gr = G * R
        lg = [lse[g * gr:(g + 1) * gr] for g in range(N_DIL)]
        mm = jnp.maximum(jnp.maximum(lg[0], lg[1]), lg[2])
        eg = [jnp.exp(l - mm) for l in lg]
        inv = 1.0 / (eg[0] + eg[1] + eg[2])
        om = sum((eg[g] * inv) * o[g * gr:(g + 1) * gr] for g in range(N_DIL))
        for i in range(G):
            h = kvh * G + i
            o_ref[:, h * HD_C:(h + 1) * HD_C] = om[i * R:(i + 1) * R]


def _attn_c_sample(mix, proj, kv, cache_k, cache_v, layer, cosq, sinq, n_prompt, DB):
    r0 = n_prompt // SAMPLE_PAD
    lb = cache_k.shape[2]
    cspec = pl.BlockSpec((None, None, lb, KV_C, HD_C), lambda b: (layer, b, 0, 0, 0))
    return pl.pallas_call(
        functools.partial(_attn_c_sample_kernel, lb=lb),
        grid=(DB,),
        in_specs=[pl.BlockSpec((SAMPLE_PAD, ODD_IN), lambda b: (r0 + b, 0)),
                  pl.BlockSpec((SAMPLE_PAD, 2 * KVW_C), lambda b: (r0 + b, 0)),
                  cspec, cspec,
                  pl.BlockSpec((SAMPLE_PAD, LANES), lambda b: (0, 0)),
                  pl.BlockSpec((SAMPLE_PAD, LANES), lambda b: (0, 0)),
                  pl.BlockSpec(memory_space=pl.ANY)],
        out_specs=pl.BlockSpec((SAMPLE_PAD, QW_C), lambda b: (r0 + b, 0)),
        out_shape=jax.ShapeDtypeStruct(mix.shape, F32),
        input_output_aliases={6: 0},
        compiler_params=_cparams(("parallel",)), name="attn_c_sample",
    )(proj, kv, cache_k, cache_v, cosq, sinq, mix)


def _row_max(v):
    return jnp.max(v, axis=1, keepdims=True)


def _first_argmax(v, m, lane):
    return jnp.min(jnp.where(v == m, lane, LANES), axis=1, keepdims=True)


def _router_kernel(x_ref, rwh_ref, rwl_ref, rb_ref, idx_ref, gate_ref, rank_ref, cnt_ref, carry_ref):
    tm = x_ref.shape[0]

    @pl.when(pl.program_id(0) == 0)
    def _():
        carry_ref[...] = jnp.zeros_like(carry_ref)

    ninf = -jnp.inf
    x = x_ref[...]
    xh = x.astype(BF16)
    xl = (x - xh.astype(F32)).astype(BF16)
    wh = rwh_ref[...]
    logits = (jnp.dot(xh, wh, preferred_element_type=F32) + jnp.dot(xl, wh, preferred_element_type=F32)
              + jnp.dot(xh, rwl_ref[...], preferred_element_type=F32))
    scores = jax.nn.sigmoid(logits)
    lane = lax.broadcasted_iota(I32, (tm, LANES), 1)
    valid = lane < N_EXPERTS
    choice = jnp.where(valid, scores + rb_ref[...], ninf)
    per_group = N_EXPERTS // N_EXPERT_GROUPS
    grp = lane // per_group
    gs = jnp.full((tm, LANES), ninf, F32)
    for g in range(N_EXPERT_GROUPS):
        vg = jnp.where(grp == g, choice, ninf)
        m1 = _row_max(vg)
        i1 = _first_argmax(vg, m1, lane)
        m2 = _row_max(jnp.where(lane == i1, ninf, vg))
        gs = jnp.where(lane == g, m1 + m2, gs)
    emask = jnp.zeros((tm, LANES), jnp.bool_)
    for _ in range(TOPK_GROUPS):
        m = _row_max(gs)
        gi = _first_argmax(gs, m, lane)
        emask = emask | (grp == gi)
        gs = jnp.where(lane == gi, ninf, gs)
    sel = jnp.where(emask & valid, choice, ninf)
    hots, idxs, gates = [], [], []
    for _ in range(TOP_K):
        m = _row_max(sel)
        ik = _first_argmax(sel, m, lane)
        hot = lane == ik
        hots.append(hot)
        idxs.append(ik)
        gates.append(jnp.sum(jnp.where(hot, scores, 0.0), axis=1, keepdims=True))
        sel = jnp.where(hot, ninf, sel)
    gsum = sum(gates)
    onehot = sum(h.astype(F32) for h in hots)
    ri = lax.broadcasted_iota(I32, (tm, tm), 0)
    ci = lax.broadcasted_iota(I32, (tm, tm), 1)
    before = jnp.dot((ci < ri).astype(BF16), onehot.astype(BF16), preferred_element_type=F32)
    base = before + carry_ref[0:1, :]
    lane8 = lax.broadcasted_iota(I32, (tm, TOP_K), 1)
    idx_o = jnp.zeros((tm, TOP_K), I32)
    gate_o = jnp.zeros((tm, TOP_K), F32)
    rank_o = jnp.zeros((tm, TOP_K), I32)
    for k in range(TOP_K):
        rk = jnp.sum(jnp.where(hots[k], base, 0.0), axis=1, keepdims=True).astype(I32)
        idx_o = jnp.where(lane8 == k, idxs[k], idx_o)
        gate_o = jnp.where(lane8 == k, gates[k] / gsum * ROUTED_SCALE, gate_o)
        rank_o = jnp.where(lane8 == k, rk, rank_o)
    idx_ref[...] = idx_o
    gate_ref[...] = gate_o
    rank_ref[...] = rank_o
    total = carry_ref[...] + jnp.sum(onehot, axis=0, keepdims=True)
    carry_ref[...] = total
    cnt_ref[...] = total


def _router(x, router_w, router_bias):
    T, D = x.shape
    tm = _pick(T, (256, 128))
    rw = jnp.concatenate([router_w, jnp.zeros((D, LANES - N_EXPERTS), F32)], axis=1)
    rwh = rw.astype(BF16)
    rwl = (rw - rwh.astype(F32)).astype(BF16)
    rb = jnp.concatenate([router_bias, jnp.zeros((LANES - N_EXPERTS,), F32)]).reshape(1, LANES)
    small = lambda: pl.BlockSpec((tm, TOP_K), lambda i: (i, 0))
    return pl.pallas_call(
        _router_kernel, grid=(T // tm,),
        in_specs=[pl.BlockSpec((tm, D), lambda i: (i, 0)),
                  pl.BlockSpec((D, LANES), lambda i: (0, 0)),
                  pl.BlockSpec((D, LANES), lambda i: (0, 0)),
                  pl.BlockSpec((1, LANES), lambda i: (0, 0))],
        out_specs=[small(), small(), small(), pl.BlockSpec((8, LANES), lambda i: (0, 0))],
        out_shape=[jax.ShapeDtypeStruct((T, TOP_K), I32), jax.ShapeDtypeStruct((T, TOP_K), F32),
                   jax.ShapeDtypeStruct((T, TOP_K), I32), jax.ShapeDtypeStruct((8, LANES), F32)],
        scratch_shapes=[pltpu.VMEM((8, LANES), F32)],
        compiler_params=_cparams(("arbitrary",)), name="router",
    )(x, rwh, rwl, rb)


def _pack_bf16_pairs(x):
    half = x.shape[1] // 2
    lo = pltpu.bitcast(x[:, :half].astype(BF16).astype(F32), U32) >> 16
    hi = pltpu.bitcast(x[:, half:].astype(BF16).astype(F32), U32) & jnp.uint32(0xFFFF0000)
    return hi | lo


def _unpack_bf16_pairs(w):
    lo = pltpu.bitcast(w << 16, F32).astype(BF16)
    hi = pltpu.bitcast(w & jnp.uint32(0xFFFF0000), F32).astype(BF16)
    return lo, hi


def _dispatch_kernel(zstart_ref, zcount_ref, x_ref, pos_hbm, xs_hbm, packed_ref, zrow_ref, pos_ref, psem, sem):
    i = pl.program_id(0)
    tm = x_ref.shape[0]
    pcopy = pltpu.make_async_copy(pos_hbm.at[i], pos_ref, psem)
    pcopy.start()
    packed_ref[...] = _pack_bf16_pairs(x_ref[...])

    def row_copy(src, r, dst_row):
        return pltpu.make_async_copy(src.at[pl.ds(r, 1)], xs_hbm.at[pl.ds(dst_row, 1)], sem)

    @pl.when(i == 0)
    def _():
        zrow_ref[...] = jnp.zeros_like(zrow_ref)

        def per_expert(e, carry):
            def start(r, c):
                row_copy(zrow_ref, 0, zstart_ref[e] + r).start()
                return c

            def wait(r, c):
                row_copy(zrow_ref, 0, 0).wait()
                return c

            lax.fori_loop(0, zcount_ref[e], start, 0)
            lax.fori_loop(0, zcount_ref[e], wait, 0)
            return carry

        lax.fori_loop(0, N_EXPERTS, per_expert, 0)

    pcopy.wait()

    def start(r, c):
        for k in range(TOP_K):
            row_copy(packed_ref, r, pos_ref[r * TOP_K + k]).start()
        return c

    def wait(r, c):
        for k in range(TOP_K):
            row_copy(packed_ref, 0, 0).wait()
        return c

    lax.fori_loop(0, tm, start, 0)
    lax.fori_loop(0, tm, wait, 0)


def _dispatch(x, pos, zstart, zcount, n_slots):
    T, D = x.shape
    tm = _pick(T, (256, 128))
    return pl.pallas_call(
        _dispatch_kernel,
        grid_spec=pltpu.PrefetchScalarGridSpec(
            num_scalar_prefetch=2, grid=(T // tm,),
            in_specs=[pl.BlockSpec((tm, D), lambda i, *_: (i, 0)),
                      pl.BlockSpec(memory_space=pl.ANY)],
            out_specs=pl.BlockSpec(memory_space=pl.ANY),
            scratch_shapes=[pltpu.VMEM((tm, D // 2), U32), pltpu.VMEM((8, D // 2), U32),
                            pltpu.SMEM((tm * TOP_K,), I32),
                            pltpu.SemaphoreType.DMA, pltpu.SemaphoreType.DMA]),
        out_shape=jax.ShapeDtypeStruct((n_slots, D // 2), U32),
        compiler_params=pltpu.CompilerParams(dimension_semantics=("arbitrary",), vmem_limit_bytes=VMEM_LIMIT,
                                             has_side_effects=True),
        name="moe_dispatch",
    )(zstart, zcount, x, pos.reshape(T // tm, tm * TOP_K))


def _experts_kernel(te_ref, nu_ref, xs_ref, w1_ref, w3_ref, w2_ref, ys_ref, w1b, w3b, w2b):
    i = pl.program_id(0)

    @pl.when(i < nu_ref[0])
    def _():
        prev = te_ref[jnp.maximum(i - 1, 0)]

        @pl.when((i == 0) | (te_ref[i] != prev))
        def _():
            w1b[...] = w1_ref[...].astype(BF16)
            w3b[...] = w3_ref[...].astype(BF16)
            w2b[...] = w2_ref[...].astype(BF16)

        lo, hi = _unpack_bf16_pairs(xs_ref[...])
        half = lo.shape[1]
        h1 = (jnp.dot(lo, w1b[:half, :], preferred_element_type=F32)
              + jnp.dot(hi, w1b[half:, :], preferred_element_type=F32))
        h3 = (jnp.dot(lo, w3b[:half, :], preferred_element_type=F32)
              + jnp.dot(hi, w3b[half:, :], preferred_element_type=F32))
        ys_ref[...] = jnp.dot((_silu(h1) * h3).astype(BF16), w2b[...], preferred_element_type=F32)


def _experts(xs, tile_e, n_used, w1, w3, w2, layer):
    n_slots = xs.shape[0]
    nt = n_slots // EXPERT_TILE
    D, DE = w1.shape[2], w1.shape[3]
    live = lambda i, te, nu: jnp.minimum(i, nu[0] - 1)
    return pl.pallas_call(
        _experts_kernel,
        grid_spec=pltpu.PrefetchScalarGridSpec(
            num_scalar_prefetch=2, grid=(nt,),
            in_specs=[pl.BlockSpec((EXPERT_TILE, D // 2), lambda i, te, nu: (live(i, te, nu), 0)),
                      pl.BlockSpec((None, None, D, DE), lambda i, te, nu: (layer, te[live(i, te, nu)], 0, 0)),
                      pl.BlockSpec((None, None, D, DE), lambda i, te, nu: (layer, te[live(i, te, nu)], 0, 0)),
                      pl.BlockSpec((None, None, DE, D), lambda i, te, nu: (layer, te[live(i, te, nu)], 0, 0))],
            out_specs=pl.BlockSpec((EXPERT_TILE, D), lambda i, te, nu: (live(i, te, nu), 0)),
            scratch_shapes=[pltpu.VMEM((D, DE), BF16), pltpu.VMEM((D, DE), BF16), pltpu.VMEM((DE, D), BF16)]),
        out_shape=jax.ShapeDtypeStruct((n_slots, D), F32),
        compiler_params=_cparams(("arbitrary",)), name="moe_experts",
    )(tile_e, n_used, xs, w1, w3, w2)


def _shared_kernel(x_ref, w1_ref, w3_ref, w2_ref, o_ref, w1b, w3b, w2b):
    @pl.when(pl.program_id(0) == 0)
    def _():
        w1b[...] = w1_ref[...].astype(BF16)
        w3b[...] = w3_ref[...].astype(BF16)
        w2b[...] = w2_ref[...].astype(BF16)

    xb = x_ref[...].astype(BF16)
    h1 = jnp.dot(xb, w1b[...], preferred_element_type=F32)
    h3 = jnp.dot(xb, w3b[...], preferred_element_type=F32)
    o_ref[...] = jnp.dot((_silu(h1) * h3).astype(BF16), w2b[...], preferred_element_type=F32)


def _shared_expert(x, w1, w3, w2, layer):
    T, D = x.shape
    DE = w1.shape[2]
    tm = _pick(T, (256, 128))
    const = lambda r, c: pl.BlockSpec((None, r, c), lambda i: (layer, 0, 0))
    return pl.pallas_call(
        _shared_kernel, grid=(T // tm,),
        in_specs=[pl.BlockSpec((tm, D), lambda i: (i, 0)), const(D, DE), const(D, DE), const(DE, D)],
        out_specs=pl.BlockSpec((tm, D), lambda i: (i, 0)),
        out_shape=jax.ShapeDtypeStruct((T, D), F32),
        scratch_shapes=[pltpu.VMEM((D, DE), BF16), pltpu.VMEM((D, DE), BF16), pltpu.VMEM((DE, D), BF16)],
        compiler_params=_cparams(("arbitrary",)), name="moe_shared",
    )(x, w1, w3, w2)


def _combine_kernel(x_ref, sh_ref, gate_ref, g_ref, b_ref, pos_hbm, ys_hbm, o_ref, buf_ref, pos_ref, psem, sem):
    i = pl.program_id(0)
    tm = x_ref.shape[0]
    pcopy = pltpu.make_async_copy(pos_hbm.at[i], pos_ref, psem)
    pcopy.start()
    pcopy.wait()

    def row_copy(src_row, k, r):
        return pltpu.make_async_copy(ys_hbm.at[pl.ds(src_row, 1)], buf_ref.at[k, pl.ds(r, 1)], sem)

    def start(r, c):
        for k in range(TOP_K):
            row_copy(pos_ref[r * TOP_K + k], k, r).start()
        return c

    def wait(r, c):
        for k in range(TOP_K):
            row_copy(0, 0, 0).wait()
        return c

    lax.fori_loop(0, tm, start, 0)
    lax.fori_loop(0, tm, wait, 0)
    gate = gate_ref[...]
    y = sh_ref[...]
    for k in range(TOP_K):
        y = y + gate[:, k:k + 1] * buf_ref[k]
    o_ref[...] = _layernorm_rows(ALPHA * x_ref[...] + y, g_ref[...], b_ref[...])


def _combine(x, shared, gate, pos, ys, g, b):
    T, D = x.shape
    tm = BLK
    row = pl.BlockSpec((tm, D), lambda i: (i, 0))
    vec = pl.BlockSpec((1, D), lambda i: (0, 0))
    return pl.pallas_call(
        _combine_kernel, grid=(T // tm,),
        in_specs=[row, row, pl.BlockSpec((tm, TOP_K), lambda i: (i, 0)), vec, vec,
                  pl.BlockSpec(memory_space=pl.ANY), pl.BlockSpec(memory_space=pl.ANY)],
        out_specs=row,
        out_shape=jax.ShapeDtypeStruct((T, D), F32),
        scratch_shapes=[pltpu.VMEM((TOP_K, tm, D), F32), pltpu.SMEM((tm * TOP_K,), I32),
                        pltpu.SemaphoreType.DMA, pltpu.SemaphoreType.DMA],
        compiler_params=_cparams(("arbitrary",)), name="moe_combine",
    )(x, shared, gate, g.reshape(1, D), b.reshape(1, D), pos.reshape(T // tm, tm * TOP_K), ys)


def _moe_ln(x, g, b, router_w, router_bias, w1, w3, w2, w1s, w3s, w2s, layer):
    T = x.shape[0]
    idx, gate, rank, cnt = _router(x, router_w, router_bias)
    te = EXPERT_TILE
    counts = cnt[0, :N_EXPERTS].astype(I32)
    padded = (counts + te - 1) // te * te
    pad_end = jnp.cumsum(padded)
    pad_start = pad_end - padded
    pos = pad_start[idx] + rank
    nt = (T * TOP_K + N_EXPERTS * (te - 1)) // te + 1
    n_used = (pad_end[-1] // te).astype(I32)
    tiles = jnp.minimum(jnp.arange(nt, dtype=I32), n_used - 1)
    tile_e = jnp.minimum(jnp.sum((pad_end[None, :] <= (tiles * te)[:, None]).astype(I32), axis=1), N_EXPERTS - 1)
    xs = _dispatch(x, pos, (pad_start + counts).astype(I32), (padded - counts).astype(I32), nt * te)
    ys = _experts(xs, tile_e, n_used.reshape(1), w1, w3, w2, layer)
    shared = _shared_expert(x, w1s, w3s, w2s, layer)
    return _combine(x, shared, gate, pos, ys, g, b)


def _ple_kernel(x_ref, xr_ref, p_ref, wg_ref, wp_ref, o_ref, xb_ref, pb_ref):
    @pl.when(pl.program_id(1) == 0)
    def _():
        xb_ref[...] = x_ref[...].astype(BF16)
        pb_ref[...] = p_ref[...].astype(BF16)

    gate = jax.nn.sigmoid(jnp.dot(xb_ref[...], wg_ref[...].astype(BF16), preferred_element_type=F32))
    proj = jnp.dot(pb_ref[...], wp_ref[...].astype(BF16), preferred_element_type=F32)
    o_ref[...] = xr_ref[...] + gate * proj


def _ple(x, p, wg, wp, layer, tn=512):
    T, D = x.shape
    P = p.shape[1]
    tm = _pick(T, (768, 512, 384, 256, 128))
    return pl.pallas_call(
        _ple_kernel, grid=(T // tm, D // tn),
        in_specs=[pl.BlockSpec((tm, D), lambda i, j: (i, 0)),
                  pl.BlockSpec((tm, tn), lambda i, j: (i, j)),
                  pl.BlockSpec((tm, P), lambda i, j: (i, 0)),
                  pl.BlockSpec((None, D, tn), lambda i, j: (layer, 0, j)),
                  pl.BlockSpec((None, P, tn), lambda i, j: (layer, 0, j))],
        out_specs=pl.BlockSpec((tm, tn), lambda i, j: (i, j)),
        out_shape=jax.ShapeDtypeStruct((T, D), F32),
        scratch_shapes=[pltpu.VMEM((tm, D), BF16), pltpu.VMEM((tm, P), BF16)],
        compiler_params=_cparams(("parallel", "arbitrary")), name="ple",
    )(x, x, p, wg, wp)


def _pad_time(a, axis=1):
    pad = [(0, 0)] * a.ndim
    pad[axis] = (0, SAMPLE_PAD - a.shape[axis])
    return jnp.pad(a, pad)


def kernel(x_prompt, x_sample, cache_a_k, cache_a_v, state_b_ssm, state_b_conv, cache_c_k, cache_c_v, p_prompt, p_sample, w_in_even, sink_a, conv_w_b, conv_b_b, dt_bias_b, a_log_b, d_skip_b, norm_w_b, w_out_even, w_in_odd, w_out_odd, ln_g, ln_b, router_w, router_bias, w1_e, w3_e, w2_e, w1_s, w3_s, w2_s, w_ple_gate, w_ple_proj):
    B, L, D = x_prompt.shape
    DB, TS, _ = x_sample.shape
    NP = B * L
    assert L % (BLK * max(d for _, d in DILATIONS)) == 0 and TS <= SAMPLE_PAD and CONV_K - 1 <= TS
    NS = DB * SAMPLE_PAD
    npb = NP // BLK

    x = jnp.concatenate([x_prompt.reshape(NP, D), _pad_time(x_sample).reshape(NS, D)], axis=0)

    prompt_pos = np.arange(L)
    sample_pos = PAST_LEN + np.arange(SAMPLE_PAD)
    all_pos = np.concatenate([prompt_pos, np.tile(sample_pos, BLK // SAMPLE_PAD)])
    tabs = {}
    for name, hd in (("a", HD_A), ("c", HD_C)):
        sc = hd ** -0.5
        tabs[name] = dict(k=_rope_tables(all_pos, hd), qp=_rope_tables(prompt_pos, hd, sc),
                          qs=_rope_tables(sample_pos, hd, sc))

    zeros_carry = jnp.zeros((B, 8, CONV_DIM), F32)
    zeros_state = jnp.zeros((B, D_INNER, D_STATE), F32)
    ssm_in = state_b_ssm.reshape(-1, D_INNER, D_STATE)

    outs = {k: [] for k in ("pa_k", "pa_v", "pb_ssm", "pb_conv", "pc_k", "pc_v",
                            "sa_k", "sa_v", "sb_ssm", "sb_conv", "sc_k", "sc_v")}
    for i in range(DEPTH):
        j = i // 2
        if i % 2 == 0:
            proj = _matmul(x, w_in_even, j)
            kv = _rope_kv(proj, *tabs["a"]["k"], npb, L // BLK, HD_A, KV_A * HD_A, OFF_K_A)
            mix = _attn_a_prompt(proj, kv, sink_a[j], *tabs["a"]["qp"], B, L)
            mix = _attn_a_sample(mix, proj, kv, cache_a_k[j], cache_a_v[j], sink_a[j], *tabs["a"]["qs"], NP, DB)
            ssm_w = (conv_w_b[j], conv_b_b[j], dt_bias_b[j], a_log_b[j], d_skip_b[j], norm_w_b[j])
            mix, hp = _ssd(mix, proj, zeros_carry, zeros_state, *ssm_w,
                           row0=0, nbatch=B, nc=L // BLK, rows=BLK, valid=BLK)
            carry_s = jnp.pad(state_b_conv[j], ((0, 0), (8 - (CONV_K - 1), 0), (0, 0)))
            mix, hs = _ssd(mix, proj, carry_s, ssm_in, *ssm_w,
                           row0=NP, nbatch=DB, nc=1, rows=SAMPLE_PAD, valid=TS, h0_base=j * DB)
            h = _matmul(mix, w_out_even, j)
            kvp = kv[:NP].reshape(B, L, 2, KV_A, HD_A)
            kvs = kv[NP:].reshape(DB, SAMPLE_PAD, 2, KV_A, HD_A)
            keep = min(WINDOW_A, L)
            outs["pa_k"].append(kvp[:, L - keep:, 0]); outs["pa_v"].append(kvp[:, L - keep:, 1])
            outs["sa_k"].append(kvs[:, :TS, 0]); outs["sa_v"].append(kvs[:, :TS, 1])
            outs["pb_ssm"].append(hp.reshape(B, H_B, HD_B, D_STATE))
            outs["sb_ssm"].append(hs.reshape(DB, H_B, HD_B, D_STATE))
            outs["pb_conv"].append(proj[:NP].reshape(B, L, EVEN_IN)[:, L - (CONV_K - 1):, OFF_XBC:OFF_DT])
            outs["sb_conv"].append(proj[NP:].reshape(DB, SAMPLE_PAD, EVEN_IN)[:, TS - (CONV_K - 1):TS, OFF_XBC:OFF_DT])
        else:
            proj = _matmul(x, w_in_odd, j)
            kv, k_heads, v_heads = _rope_kv(proj, *tabs["c"]["k"], npb, L // BLK, HD_C, KVW_C, OFF_K_C, per_head=True)
            go, gl = [], []
            for g, (win, dil) in enumerate(DILATIONS):
                assert win // dil == BLK
                if dil == 1:
                    o_g, l_g = _attn_c_prompt(proj, kv, *tabs["c"]["qp"], g, B, L)
                else:
                    o_g, l_g = _attn_c_dilated(proj, kv, *tabs["c"]["qp"], g, dil, B, L)
                go.append(o_g); gl.append(l_g)
            mix = _mix_c(go, gl, NP)
            mix = _attn_c_sample(mix, proj, kv, cache_c_k, cache_c_v, j, *tabs["c"]["qs"], NP, DB)
            h = _matmul(mix, w_out_odd, j)
            keep = min(max(w for w, _ in DILATIONS), L)
            outs["pc_k"].append(k_heads[:NP].reshape(B, L, KV_C, HD_C)[:, L - keep:])
            outs["pc_v"].append(v_heads[:NP].reshape(B, L, KV_C, HD_C)[:, L - keep:])
            outs["sc_k"].append(k_heads[NP:].reshape(DB, SAMPLE_PAD, KV_C, HD_C)[:, :TS])
            outs["sc_v"].append(v_heads[NP:].reshape(DB, SAMPLE_PAD, KV_C, HD_C)[:, :TS])
        x = _add_ln(x, h, ln_g[i, 0], ln_b[i, 0])
        x = _moe_ln(x, ln_g[i, 1], ln_b[i, 1], router_w[i], router_bias[i], w1_e, w3_e, w2_e,
                    w1_s, w3_s, w2_s, i)
        p = jnp.concatenate([p_prompt[i].reshape(NP, PLE_DIM), _pad_time(p_sample[i]).reshape(NS, PLE_DIM)], axis=0)
        x = _ple(x, p, w_ple_gate, w_ple_proj, i)

    st = lambda k: jnp.stack(outs[k])
    y_prompt = x[:NP].reshape(B, L, D)
    y_sample = x[NP:].reshape(DB, SAMPLE_PAD, D)[:, :TS]
    return (y_prompt, y_sample, st("pa_k"), st("pa_v"), st("pb_ssm"), st("pb_conv"), st("pc_k"), st("pc_v"),
            st("sa_k"), st("sa_v"), st("sb_ssm"), st("sb_conv"), st("sc_k"), st("sc_v"))
```

```python
import functools
import math

import numpy as np
import jax
import jax.numpy as jnp
from jax import lax
from jax.experimental import pallas as pl
from jax.experimental.pallas import tpu as pltpu

F32 = jnp.float32
BF16 = jnp.bfloat16
I32 = jnp.int32
U32 = jnp.uint32

D_MODEL = 2048
DEPTH = 4
PAST_LEN = 16384
ALPHA = (2.0 * DEPTH) ** 0.25
LN_EPS = 1e-5
RMS_EPS = 1e-5
ROPE_THETA = 10000.0
BLK = 128

HD_A = 64
H_A = D_MODEL // (2 * HD_A)
KV_A = H_A // 8
WINDOW_A = 128

HD_B = 64
D_INNER = D_MODEL // 2
H_B = D_INNER // HD_B
NG_B = 2
D_STATE = 128
CONV_K = 4
CONV_DIM = D_INNER + 2 * NG_B * D_STATE
EVEN_IN = H_A * HD_A + 2 * KV_A * HD_A + D_INNER + CONV_DIM + H_B
OFF_K_A = H_A * HD_A
OFF_V_A = OFF_K_A + KV_A * HD_A
OFF_Z = OFF_V_A + KV_A * HD_A
OFF_XBC = OFF_Z + D_INNER
OFF_DT = OFF_XBC + CONV_DIM

HD_C = 128
H_C = D_MODEL // HD_C
KV_C = 4
DILATIONS = ((128, 1), (512, 4), (2048, 16))
N_DIL = len(DILATIONS)
QW_C = H_C * HD_C
KVW_C = KV_C * HD_C
ODD_IN = N_DIL * QW_C + 2 * KVW_C
OFF_K_C = N_DIL * QW_C
OFF_V_C = OFF_K_C + KVW_C

N_EXPERTS = 64
N_EXPERT_GROUPS = 8
TOPK_GROUPS = 4
TOP_K = 8
D_EXPERT = D_MODEL // 4
ROUTED_SCALE = 2.5
PLE_DIM = 256

SAMPLE_PAD = 8
LANES = 128
VMEM_LIMIT = 56 * 1024 * 1024
EXPERT_TILE = 256
NEG = -1e30


def _cparams(sem, vmem=None):
    return pltpu.CompilerParams(dimension_semantics=sem, vmem_limit_bytes=vmem or VMEM_LIMIT)


def _pick(n, cands):
    for c in cands:
        if n % c == 0:
            return c
    raise ValueError(f"no tile for {n}")


def _mm_kernel(x_ref, w_ref, o_ref, xb_ref):
    @pl.when(pl.program_id(1) == 0)
    def _():
        xb_ref[...] = x_ref[...].astype(BF16)

    o_ref[...] = jnp.dot(xb_ref[...], w_ref[...].astype(BF16), preferred_element_type=F32)


def _matmul(x, w, layer, tn=512):
    M, K = x.shape
    N = w.shape[2]
    tm = _pick(M, (768, 512, 384, 256, 128))
    return pl.pallas_call(
        _mm_kernel,
        grid=(M // tm, pl.cdiv(N, tn)),
        in_specs=[pl.BlockSpec((tm, K), lambda i, j: (i, 0)),
                  pl.BlockSpec((None, K, tn), lambda i, j: (layer, 0, j))],
        out_specs=pl.BlockSpec((tm, tn), lambda i, j: (i, j)),
        out_shape=jax.ShapeDtypeStruct((M, N), F32),
        scratch_shapes=[pltpu.VMEM((tm, K), BF16)],
        compiler_params=_cparams(("parallel", "arbitrary")),
        name="matmul",
    )(x, w)


def _layernorm_rows(v, g, b):
    mu = jnp.mean(v, axis=-1, keepdims=True)
    vc = v - mu
    var = jnp.mean(vc * vc, axis=-1, keepdims=True)
    return vc * lax.rsqrt(var + LN_EPS) * g + b


def _mm_ln_kernel(a_ref, w_ref, x_ref, g_ref, b_ref, o_ref, acc_ref):
    k = pl.program_id(1)

    @pl.when(k == 0)
    def _():
        acc_ref[...] = jnp.zeros_like(acc_ref)

    acc_ref[...] += jnp.dot(a_ref[...].astype(BF16), w_ref[...].astype(BF16), preferred_element_type=F32)

    @pl.when(k == pl.num_programs(1) - 1)
    def _():
        o_ref[...] = _layernorm_rows(ALPHA * x_ref[...] + acc_ref[...], g_ref[...], b_ref[...])


def _matmul_add_ln(a, w, layer, x, g, b, tk=512):
    T, K = a.shape
    D = w.shape[2]
    tm = _pick(T, (768, 512, 384, 256, 128))
    row = pl.BlockSpec((tm, D), lambda i, k: (i, 0))
    vec = pl.BlockSpec((1, D), lambda i, k: (0, 0))
    return pl.pallas_call(
        _mm_ln_kernel, grid=(T // tm, K // tk),
        in_specs=[pl.BlockSpec((tm, tk), lambda i, k: (i, k)),
                  pl.BlockSpec((None, tk, D), lambda i, k: (layer, k, 0)),
                  row, vec, vec],
        out_specs=row,
        out_shape=jax.ShapeDtypeStruct((T, D), F32),
        scratch_shapes=[pltpu.VMEM((tm, D), F32)],
        compiler_params=_cparams(("parallel", "arbitrary")), name="matmul_add_ln",
    )(a, w, x, g.reshape(1, D), b.reshape(1, D))


def _rope_tables(pos, head_dim, scale=1.0):
    half = head_dim // 2
    inv = ROPE_THETA ** (-np.arange(half, dtype=np.float64) / half)
    ang = np.asarray(pos, np.float64)[:, None] * inv
    cos, sin = np.cos(ang), np.sin(ang)
    reps = LANES // head_dim
    cos_t = np.tile(np.concatenate([cos, cos], -1), (1, reps)) * scale
    sin_t = np.tile(np.concatenate([-sin, sin], -1), (1, reps)) * scale
    return jnp.asarray(cos_t, F32), jnp.asarray(sin_t, F32)


def _rope_chunk(x, cos, sin, head_dim):
    half = head_dim // 2
    if head_dim == LANES:
        rot = pltpu.roll(x, half, axis=1)
    else:
        lane = lax.broadcasted_iota(I32, x.shape, 1)
        first = (lane & (head_dim - 1)) < half
        rot = jnp.where(first, pltpu.roll(x, LANES - half, axis=1), pltpu.roll(x, half, axis=1))
    return x * cos + rot * sin


def _rope_kv_kernel(k_ref, v_ref, cos_ref, sin_ref, o_ref, *head_refs, head_dim, width):
    cos, sin = cos_ref[...], sin_ref[...]
    for c in range(width // LANES):
        sl = slice(c * LANES, (c + 1) * LANES)
        kr = _rope_chunk(k_ref[:, sl], cos, sin, head_dim)
        o_ref[:, sl] = kr
        if head_refs:
            nh = width // LANES
            head_refs[0][pl.ds(c, BLK, stride=nh), :] = kr
            head_refs[1][pl.ds(c, BLK, stride=nh), :] = v_ref[:, sl]
    o_ref[:, width:] = v_ref[...]


def _rope_kv(proj, cos_all, sin_all, n_prompt_blocks, pos_blocks, head_dim, width, k_off, per_head=False):
    T = proj.shape[0]
    kb = k_off // width
    tab = lambda i: (jnp.where(i < n_prompt_blocks, i % pos_blocks, pos_blocks), 0)
    out_specs = [pl.BlockSpec((BLK, 2 * width), lambda i: (i, 0))]
    out_shape = [jax.ShapeDtypeStruct((T, 2 * width), F32)]
    if per_head:
        assert head_dim == LANES
        nh = width // LANES
        out_specs += [pl.BlockSpec((BLK * nh, LANES), lambda i: (i, 0))] * 2
        out_shape += [jax.ShapeDtypeStruct((T * nh, LANES), F32)] * 2
    res = pl.pallas_call(
        functools.partial(_rope_kv_kernel, head_dim=head_dim, width=width),
        grid=(T // BLK,),
        in_specs=[pl.BlockSpec((BLK, width), lambda i: (i, kb)),
                  pl.BlockSpec((BLK, width), lambda i: (i, kb + 1)),
                  pl.BlockSpec((BLK, LANES), tab), pl.BlockSpec((BLK, LANES), tab)],
        out_specs=out_specs, out_shape=out_shape,
        compiler_params=_cparams(("parallel",)), name="rope_kv",
    )(proj, proj, cos_all, sin_all)
    return res if per_head else res[0]


def _softmax_pv(s, vv, sink_col=None):
    m = jnp.max(s, axis=1, keepdims=True)
    if sink_col is not None:
        m = jnp.maximum(m, sink_col)
    e = jnp.exp(s - m)
    den = jnp.sum(e, axis=1, keepdims=True)
    if sink_col is not None:
        den = den + jnp.exp(sink_col - m)
    o = jnp.dot(e.astype(BF16), vv, preferred_element_type=F32) / den
    return o, m + jnp.log(den)


def _qk(qs, kk):
    return lax.dot_general(qs, kk, (((1,), (1,)), ((), ())), preferred_element_type=F32)


def _band_mask(rows, first_block):
    qi = lax.broadcasted_iota(I32, (rows, 2 * BLK), 0) & (BLK - 1)
    kj = lax.broadcasted_iota(I32, (rows, 2 * BLK), 1)
    ok = (kj >= qi) & (kj <= qi + BLK)
    return ok & ((kj >= BLK) | jnp.logical_not(first_block))


def _dup_half(x, which):
    lane = lax.broadcasted_iota(I32, x.shape, 1)
    sw = pltpu.roll(x, HD_A, axis=1)
    return jnp.where(lane < HD_A, x, sw) if which == 0 else jnp.where(lane < HD_A, sw, x)


def _attn_a_heads(q_chunks, kdup, vdup, mask, sink_ref, kvh):
    R = q_chunks[0].shape[0]
    lane = lax.broadcasted_iota(I32, (R, LANES), 1)
    lo = lane < HD_A
    parts, sinks = [], []
    for c, ch in enumerate(q_chunks):
        parts.append(jnp.where(lo, ch, 0.0))
        parts.append(jnp.where(lo, 0.0, ch))
        for par in range(2):
            sinks.append(jnp.full((R, 1), sink_ref[kvh * 8 + 2 * c + par], F32))
    qs = jnp.concatenate(parts, axis=0).astype(BF16)
    s = jnp.where(mask, _qk(qs, kdup), NEG)
    o, _ = _softmax_pv(s, vdup, jnp.concatenate(sinks, axis=0))
    return [jnp.where(lo, o[(2 * c) * R:(2 * c + 1) * R], o[(2 * c + 1) * R:(2 * c + 2) * R])
            for c in range(len(q_chunks))]


def _attn_a_prompt_kernel(sink_ref, q_ref, kvp_ref, kvc_ref, cos_ref, sin_ref, o_ref):
    first = pl.program_id(1) == 0
    cos, sin = cos_ref[...], sin_ref[...]
    kvp, kvc = kvp_ref[...], kvc_ref[...]
    kk = jnp.concatenate([kvp[:, :LANES], kvc[:, :LANES]], axis=0)
    vv = jnp.concatenate([kvp[:, LANES:], kvc[:, LANES:]], axis=0)
    mask = _band_mask(8 * BLK, first)
    for kvh in range(KV_A):
        kdup = _dup_half(kk, kvh).astype(BF16)
        vdup = _dup_half(vv, kvh).astype(BF16)
        chunks = [_rope_chunk(q_ref[:, (kvh * 4 + c) * LANES:(kvh * 4 + c + 1) * LANES], cos, sin, HD_A)
                  for c in range(4)]
        outs = _attn_a_heads(chunks, kdup, vdup, mask, sink_ref, kvh)
        for c in range(4):
            o_ref[:, (kvh * 4 + c) * LANES:(kvh * 4 + c + 1) * LANES] = outs[c]


def _attn_a_prompt(proj, kv, sink, cosq, sinq, B, L):
    T = proj.shape[0]
    nb = L // BLK
    return pl.pallas_call(
        _attn_a_prompt_kernel,
        grid=(B, nb),
        in_specs=[pl.BlockSpec(memory_space=pltpu.SMEM),
                  pl.BlockSpec((BLK, OFF_K_A), lambda b, m: (b * nb + m, 0)),
                  pl.BlockSpec((BLK, 2 * LANES), lambda b, m: (b * nb + jnp.maximum(m - 1, 0), 0)),
                  pl.BlockSpec((BLK, 2 * LANES), lambda b, m: (b * nb + m, 0)),
                  pl.BlockSpec((BLK, LANES), lambda b, m: (m, 0)),
                  pl.BlockSpec((BLK, LANES), lambda b, m: (m, 0))],
        out_specs=pl.BlockSpec((BLK, OFF_K_A), lambda b, m: (b * nb + m, 0)),
        out_shape=jax.ShapeDtypeStruct((T, D_MODEL), F32),
        compiler_params=_cparams(("parallel", "parallel")), name="attn_a_prompt",
    )(sink, proj, kv, kv, cosq, sinq)


def _attn_a_sample_kernel(sink_ref, q_ref, kvn_ref, ck_ref, cv_ref, cos_ref, sin_ref, mix_ref, o_ref):
    del mix_ref
    R = SAMPLE_PAD
    cos, sin = cos_ref[...], sin_ref[...]
    kvn = kvn_ref[...]
    zpad = jnp.zeros((BLK - R, LANES), F32)
    kk = jnp.concatenate([ck_ref[...], kvn[:, :LANES], zpad], axis=0)
    vv = jnp.concatenate([cv_ref[...], kvn[:, LANES:], zpad], axis=0)
    t = lax.broadcasted_iota(I32, (8 * R, 2 * BLK), 0) & (R - 1)
    j = lax.broadcasted_iota(I32, (8 * R, 2 * BLK), 1)
    mask = ((j < BLK) & (j >= t)) | ((j >= BLK) & (j - BLK <= t) & (j < BLK + R))
    for kvh in range(KV_A):
        kdup = _dup_half(kk, kvh).astype(BF16)
        vdup = _dup_half(vv, kvh).astype(BF16)
        chunks = [_rope_chunk(q_ref[:, (kvh * 4 + c) * LANES:(kvh * 4 + c + 1) * LANES], cos, sin, HD_A)
                  for c in range(4)]
        outs = _attn_a_heads(chunks, kdup, vdup, mask, sink_ref, kvh)
        for c in range(4):
            o_ref[:, (kvh * 4 + c) * LANES:(kvh * 4 + c + 1) * LANES] = outs[c]


def _attn_a_sample(mix, proj, kv, cache_k, cache_v, sink, cosq, sinq, n_prompt, DB):
    r0 = n_prompt // SAMPLE_PAD
    lb = cache_k.shape[1]
    assert lb == BLK
    return pl.pallas_call(
        _attn_a_sample_kernel,
        grid=(DB,),
        in_specs=[pl.BlockSpec(memory_space=pltpu.SMEM),
                  pl.BlockSpec((SAMPLE_PAD, OFF_K_A), lambda b: (r0 + b, 0)),
                  pl.BlockSpec((SAMPLE_PAD, 2 * LANES), lambda b: (r0 + b, 0)),
                  pl.BlockSpec((None, lb, LANES), lambda b: (b, 0, 0)),
                  pl.BlockSpec((None, lb, LANES), lambda b: (b, 0, 0)),
                  pl.BlockSpec((SAMPLE_PAD, LANES), lambda b: (0, 0)),
                  pl.BlockSpec((SAMPLE_PAD, LANES), lambda b: (0, 0)),
                  pl.BlockSpec(memory_space=pl.ANY)],
        out_specs=pl.BlockSpec((SAMPLE_PAD, OFF_K_A), lambda b: (r0 + b, 0)),
        out_shape=jax.ShapeDtypeStruct(mix.shape, F32),
        input_output_aliases={7: 0},
        compiler_params=_cparams(("parallel",)), name="attn_a_sample",
    )(sink, proj, kv, cache_k.reshape(DB, lb, LANES), cache_v.reshape(DB, lb, LANES), cosq, sinq, mix)


def _hi_dot(a, b):
    return jnp.dot(a, b, precision=lax.Precision.HIGHEST, preferred_element_type=F32)


def _silu(x):
    return x * jax.nn.sigmoid(x)


def _ssd_kernel(proj_ref, carry0_ref, h0_ref, cw_ref, cb_ref, dtb_ref, alog_ref, dskip_ref, nw_ref,
                exp_ref, mix_ref, y_ref, hout_ref, carry_ref, state_ref, *, rows, valid, nc):
    del mix_ref
    c = pl.program_id(1)
    Q = BLK

    @pl.when(c == 0)
    def _():
        carry_ref[...] = carry0_ref[...]
        state_ref[...] = h0_ref[...].T

    blk = proj_ref[...]
    if rows < Q:
        blk = jnp.concatenate([blk, jnp.zeros((Q - rows, blk.shape[1]), F32)], axis=0)
    z = blk[:, OFF_Z:OFF_XBC]
    xbc = blk[:, OFF_XBC:OFF_DT]
    dt_raw = blk[:, OFF_DT:OFF_DT + H_B]

    ext = jnp.concatenate([carry_ref[...], xbc], axis=0)
    u = cb_ref[...] + xbc * cw_ref[CONV_K - 1:CONV_K, :]
    for s in range(1, CONV_K):
        u = u + pltpu.roll(ext, s, axis=0)[8:] * cw_ref[CONV_K - 1 - s:CONV_K - s, :]
    carry_ref[...] = xbc[Q - 8:]
    u = _silu(u)
    xs = u[:, :D_INNER]
    bmat = u[:, D_INNER:D_INNER + NG_B * D_STATE]
    cmat = u[:, D_INNER + NG_B * D_STATE:]

    row_q = lax.broadcasted_iota(I32, (Q, LANES), 0)
    lane = lax.broadcasted_iota(I32, (Q, LANES), 1)
    dt_pad = jnp.concatenate([dt_raw, jnp.zeros((Q, LANES - H_B), F32)], axis=1)
    dtv = jnp.where((lane < H_B) & (row_q < valid), jax.nn.softplus(dt_pad + dtb_ref[...]), 0.0)
    adt = dtv * (-jnp.exp(alog_ref[...]))
    li = lax.broadcasted_iota(I32, (Q, Q), 0)
    si = lax.broadcasted_iota(I32, (Q, Q), 1)
    causal = si <= li
    acum = _hi_dot(causal.astype(F32), adt)
    acum_t = acum.T
    a_last = acum[Q - 1:Q, :]

    expand = exp_ref[...]
    xdt = xs * _hi_dot(dtv, expand)
    in_decay = _hi_dot(jnp.exp(acum), expand)
    to_end = _hi_dot(jnp.exp(a_last - acum), expand)
    chunk_decay = in_decay[Q - 1:Q, :]

    state = state_ref[...]
    state_b = state.astype(BF16)
    lo = lane < HD_B
    y_groups, new_state = [], []
    hpg = H_B // NG_B
    for g in range(NG_B):
        bg = bmat[:, g * D_STATE:(g + 1) * D_STATE]
        cg = cmat[:, g * D_STATE:(g + 1) * D_STATE].astype(BF16)
        cbm = _qk(cg, bg.astype(BF16))
        ch0, ch1 = g * hpg * HD_B, (g + 1) * hpg * HD_B
        pairs = []
        for pr in range(hpg // 2):
            col = ch0 + pr * LANES
            xpair = xdt[:, col:col + LANES].astype(BF16)
            ys = []
            for par in range(2):
                h = g * hpg + 2 * pr + par
                seg = acum[:, h:h + 1] - acum_t[h:h + 1, :]
                dec = jnp.exp(jnp.where(causal, seg, NEG))
                ys.append(jnp.dot((cbm * dec).astype(BF16), xpair, preferred_element_type=F32))
            pairs.append(jnp.where(lo, ys[0], ys[1]))
        y_off = jnp.dot(cg, state_b[:, ch0:ch1], preferred_element_type=F32)
        y_groups.append(jnp.concatenate(pairs, axis=1) + y_off * in_decay[:, ch0:ch1])
        xe = (xdt[:, ch0:ch1] * to_end[:, ch0:ch1]).astype(BF16)
        new_state.append(jnp.dot(bg.T.astype(BF16), xe, preferred_element_type=F32))
    y = jnp.concatenate(y_groups, axis=1) + xs * dskip_ref[...]
    state_ref[...] = state * chunk_decay + jnp.concatenate(new_state, axis=1)

    ug = y * _silu(z)
    gw = D_INNER // NG_B
    outs = []
    for g in range(NG_B):
        v = ug[:, g * gw:(g + 1) * gw]
        outs.append(v * lax.rsqrt(jnp.mean(v * v, axis=1, keepdims=True) + RMS_EPS))
    yn = jnp.concatenate(outs, axis=1) * nw_ref[...]
    y_ref[...] = yn[:rows]

    @pl.when(c == nc - 1)
    def _():
        hout_ref[...] = state_ref[...].T


def _ssd(mix, proj, carry0, h0, conv_w, conv_b, dt_bias, a_log, d_skip, norm_w, *, row0, nbatch, nc, rows, valid,
         h0_base=0):
    expand = jnp.asarray(np.concatenate([np.repeat(np.eye(H_B, dtype=np.float32), HD_B, axis=1),
                                         np.zeros((LANES - H_B, D_INNER), np.float32)], axis=0))
    pad_h = lambda v: jnp.concatenate([v, jnp.zeros((LANES - H_B,), F32)]).reshape(1, LANES)
    cw = jnp.concatenate([conv_w, jnp.zeros((8 - CONV_K, CONV_DIM), F32)], axis=0)
    rb0 = row0 // rows
    vec = lambda n: pl.BlockSpec((1, n), lambda b, c: (0, 0))
    y_new, h_new = pl.pallas_call(
        functools.partial(_ssd_kernel, rows=rows, valid=valid, nc=nc),
        grid=(nbatch, nc),
        in_specs=[pl.BlockSpec((rows, EVEN_IN), lambda b, c: (rb0 + b * nc + c, 0)),
                  pl.BlockSpec((None, 8, CONV_DIM), lambda b, c: (b, 0, 0)),
                  pl.BlockSpec((None, D_INNER, D_STATE), lambda b, c: (h0_base + b, 0, 0)),
                  pl.BlockSpec((8, CONV_DIM), lambda b, c: (0, 0)),
                  vec(CONV_DIM), vec(LANES), vec(LANES), vec(D_INNER), vec(D_INNER),
                  pl.BlockSpec((LANES, D_INNER), lambda b, c: (0, 0)),
                  pl.BlockSpec(memory_space=pl.ANY)],
        out_specs=[pl.BlockSpec((rows, D_INNER), lambda b, c: (rb0 + b * nc + c, 1)),
                   pl.BlockSpec((None, D_INNER, D_STATE), lambda b, c: (b, 0, 0))],
        out_shape=[jax.ShapeDtypeStruct(mix.shape, F32),
                   jax.ShapeDtypeStruct((nbatch, D_INNER, D_STATE), F32)],
        scratch_shapes=[pltpu.VMEM((8, CONV_DIM), F32), pltpu.VMEM((D_STATE, D_INNER), F32)],
        input_output_aliases={10: 0},
        compiler_params=_cparams(("parallel", "arbitrary")), name="ssd",
    )(proj, carry0, h0, cw, conv_b.reshape(1, -1), pad_h(dt_bias),
      pad_h(a_log), jnp.repeat(d_skip, HD_B).reshape(1, -1), norm_w.reshape(1, -1), expand, mix)
    return y_new, h_new.reshape(nbatch, H_B, HD_B, D_STATE)


def _attn_c_prompt_kernel(q0_ref, q1_ref, q2_ref, q3_ref, kvp_ref, kvc_ref, cos_ref, sin_ref, o_ref, lse_ref):
    first = pl.program_id(1) == 0
    cos, sin = cos_ref[...], sin_ref[...]
    kvp, kvc = kvp_ref[...], kvc_ref[...]
    G = H_C // KV_C
    mask = _band_mask(G * BLK, first)
    lane = lax.broadcasted_iota(I32, (BLK, LANES), 1)
    lse_tile = jnp.zeros((BLK, LANES), F32)
    for kvh, q_ref in enumerate((q0_ref, q1_ref, q2_ref, q3_ref)):
        ks = slice(kvh * HD_C, (kvh + 1) * HD_C)
        vs = slice(KVW_C + kvh * HD_C, KVW_C + (kvh + 1) * HD_C)
        kk = jnp.concatenate([kvp[:, ks], kvc[:, ks]], axis=0).astype(BF16)
        vv = jnp.concatenate([kvp[:, vs], kvc[:, vs]], axis=0).astype(BF16)
        qs = jnp.concatenate([_rope_chunk(q_ref[:, i * HD_C:(i + 1) * HD_C], cos, sin, HD_C) for i in range(G)],
                             axis=0).astype(BF16)
        o, lse = _softmax_pv(jnp.where(mask, _qk(qs, kk), NEG), vv)
        for i in range(G):
            h = kvh * G + i
            o_ref[:, h * HD_C:(h + 1) * HD_C] = o[i * BLK:(i + 1) * BLK]
            lse_tile = jnp.where(lane == h, lse[i * BLK:(i + 1) * BLK], lse_tile)
    lse_ref[...] = lse_tile


def _attn_c_prompt(proj, kv, cosq, sinq, g, B, L):
    T = proj.shape[0]
    nb = L // BLK
    G = H_C // KV_C
    qspec = lambda kvh: pl.BlockSpec((BLK, G * HD_C), lambda b, m: (b * nb + m, g * KV_C + kvh))
    return pl.pallas_call(
        _attn_c_prompt_kernel,
        grid=(B, nb),
        in_specs=[qspec(0), qspec(1), qspec(2), qspec(3),
                  pl.BlockSpec((BLK, 2 * KVW_C), lambda b, m: (b * nb + jnp.maximum(m - 1, 0), 0)),
                  pl.BlockSpec((BLK, 2 * KVW_C), lambda b, m: (b * nb + m, 0)),
                  pl.BlockSpec((BLK, LANES), lambda b, m: (m, 0)),
                  pl.BlockSpec((BLK, LANES), lambda b, m: (m, 0))],
        out_specs=[pl.BlockSpec((BLK, QW_C), lambda b, m: (b * nb + m, 0)),
                   pl.BlockSpec((BLK, LANES), lambda b, m: (b * nb + m, 0))],
        out_shape=[jax.ShapeDtypeStruct((T, QW_C), F32), jax.ShapeDtypeStruct((T, LANES), F32)],
        compiler_params=_cparams(("parallel", "parallel")), name="attn_c_prompt_d1",
    )(proj, proj, proj, proj, kv, kv, cosq, sinq)


def _attn_c_dilated_kernel(q0_ref, q1_ref, q2_ref, q3_ref, kp_ref, kc_ref, vp_ref, vc_ref, cos_ref, sin_ref,
                           o_ref, lse_ref, osc_ref, *, dil):
    first = pl.program_id(1) == 0
    kvh = pl.program_id(2)
    q_refs = (q0_ref, q1_ref, q2_ref, q3_ref)
    G = len(q_refs)
    mask = _band_mask(G * BLK, first)
    lane = lax.broadcasted_iota(I32, (BLK, LANES), 1)

    @pl.when(kvh == 0)
    def _():
        lse_ref[...] = jnp.zeros_like(lse_ref)

    for r in range(dil):
        rows = pl.ds(r, BLK, stride=dil)
        cos, sin = cos_ref[rows, :], sin_ref[rows, :]
        kk = jnp.concatenate([kp_ref[rows, :], kc_ref[rows, :]], axis=0).astype(BF16)
        vv = jnp.concatenate([vp_ref[rows, :], vc_ref[rows, :]], axis=0).astype(BF16)
        qs = jnp.concatenate([_rope_chunk(q_ref[rows, :], cos, sin, HD_C) for q_ref in q_refs], axis=0).astype(BF16)
        o, lse = _softmax_pv(jnp.where(mask, _qk(qs, kk), NEG), vv)
        lse_tile = lse_ref[rows, :]
        for i in range(G):
            osc_ref[i, rows, :] = o[i * BLK:(i + 1) * BLK]
            lse_tile = jnp.where(lane == kvh * G + i, lse[i * BLK:(i + 1) * BLK], lse_tile)
        lse_ref[rows, :] = lse_tile
    for i in range(G):
        o_ref[:, i * HD_C:(i + 1) * HD_C] = osc_ref[i]


def _attn_c_dilated(proj, kv, cosq, sinq, g, dil, B, L):
    T = proj.shape[0]
    R = BLK * dil
    nb = L // R
    G = H_C // KV_C
    vb = KVW_C // HD_C
    prev = lambda b, m: b * nb + jnp.maximum(m - 1, 0)
    qspec = lambda i: pl.BlockSpec((R, HD_C), lambda b, m, h: (b * nb + m, g * H_C + h * G + i))
    return pl.pallas_call(
        functools.partial(_attn_c_dilated_kernel, dil=dil),
        grid=(B, nb, KV_C),
        in_specs=[qspec(0), qspec(1), qspec(2), qspec(3),
                  pl.BlockSpec((R, HD_C), lambda b, m, h: (prev(b, m), h)),
                  pl.BlockSpec((R, HD_C), lambda b, m, h: (b * nb + m, h)),
                  pl.BlockSpec((R, HD_C), lambda b, m, h: (prev(b, m), vb + h)),
                  pl.BlockSpec((R, HD_C), lambda b, m, h: (b * nb + m, vb + h)),
                  pl.BlockSpec((R, LANES), lambda b, m, h: (m, 0)),
                  pl.BlockSpec((R, LANES), lambda b, m, h: (m, 0))],
        out_specs=[pl.BlockSpec((R, G * HD_C), lambda b, m, h: (b * nb + m, h)),
                   pl.BlockSpec((R, LANES), lambda b, m, h: (b * nb + m, 0))],
        out_shape=[jax.ShapeDtypeStruct((T, QW_C), F32), jax.ShapeDtypeStruct((T, LANES), F32)],
        scratch_shapes=[pltpu.VMEM((G, R, HD_C), F32)],
        compiler_params=_cparams(("parallel", "parallel", "arbitrary")), name=f"attn_c_prompt_d{dil}",
    )(proj, proj, proj, proj, kv, kv, kv, kv, cosq, sinq)


def _mix_c_kernel(o0_ref, o1_ref, o2_ref, l0_ref, l1_ref, l2_ref, o_ref):
    ls = [l0_ref[...], l1_ref[...], l2_ref[...]]
    m = jnp.maximum(jnp.maximum(ls[0], ls[1]), ls[2])
    es = [jnp.exp(l - m) for l in ls]
    inv = 1.0 / (es[0] + es[1] + es[2])
    ws = [e * inv for e in es]
    for h in range(H_C):
        sl = slice(h * HD_C, (h + 1) * HD_C)
        o_ref[:, sl] = (ws[0][:, h:h + 1] * o0_ref[:, sl] + ws[1][:, h:h + 1] * o1_ref[:, sl]
                        + ws[2][:, h:h + 1] * o2_ref[:, sl])


def _mix_c(outs, lses, n_prompt):
    T = outs[0].shape[0]
    tm = BLK
    ospec = pl.BlockSpec((tm, QW_C), lambda i: (i, 0))
    lspec = pl.BlockSpec((tm, LANES), lambda i: (i, 0))
    return pl.pallas_call(
        _mix_c_kernel, grid=(n_prompt // tm,),
        in_specs=[ospec] * 3 + [lspec] * 3, out_specs=ospec,
        out_shape=jax.ShapeDtypeStruct((T, QW_C), F32),
        compiler_params=_cparams(("parallel",)), name="mix_c",
    )(*outs, *lses)


def _attn_c_sample_kernel(q_ref, kvn_ref, ck_ref, cv_ref, cos_ref, sin_ref, mix_ref, o_ref, *, lb):
    del mix_ref
    R = SAMPLE_PAD
    G = H_C // KV_C
    cos, sin = cos_ref[...], sin_ref[...]
    kvn = kvn_ref[...]
    zpad = jnp.zeros((BLK - R, HD_C), F32)
    nrow = N_DIL * G * R
    rho = lax.broadcasted_iota(I32, (nrow, 1), 0)
    t = rho & (R - 1)
    grp = rho // (G * R)
    dil_m1 = jnp.where(grp == 0, DILATIONS[0][1] - 1, jnp.where(grp == 1, DILATIONS[1][1] - 1, DILATIONS[2][1] - 1))
    win = jnp.where(grp == 0, DILATIONS[0][0], jnp.where(grp == 1, DILATIONS[1][0], DILATIONS[2][0]))
    jc = lax.broadcasted_iota(I32, (nrow, lb), 1)
    dc = lb + t - jc
    mask_c = ((dc & dil_m1) == 0) & (dc <= win)
    jn = lax.broadcasted_iota(I32, (nrow, BLK), 1)
    dn = t - jn
    mask_n = (dn >= 0) & ((dn & dil_m1) == 0) & (jn < R)
    for kvh in range(KV_C):
        ks = slice(kvh * HD_C, (kvh + 1) * HD_C)
        vs = slice(KVW_C + kvh * HD_C, KVW_C + (kvh + 1) * HD_C)
        kc = ck_ref[pl.ds(kvh, lb, stride=KV_C), :].astype(BF16)
        vc = cv_ref[pl.ds(kvh, lb, stride=KV_C), :].astype(BF16)
        kn = jnp.concatenate([kvn[:, ks], zpad], axis=0).astype(BF16)
        vn = jnp.concatenate([kvn[:, vs], zpad], axis=0).astype(BF16)
        parts = []
        for g in range(N_DIL):
            for i in range(G):
                c0 = g * QW_C + (kvh * G + i) * HD_C
                parts.append(_rope_chunk(q_ref[:, c0:c0 + HD_C], cos, sin, HD_C))
        qs = jnp.concatenate(parts, axis=0).astype(BF16)
        sc = jnp.where(mask_c, _qk(qs, kc), NEG)
        sn = jnp.where(mask_n, _qk(qs, kn), NEG)
        m = jnp.maximum(jnp.max(sc, axis=1, keepdims=True), jnp.max(sn, axis=1, keepdims=True))
        ec, en = jnp.exp(sc - m), jnp.exp(sn - m)
        den = jnp.sum(ec, axis=1, keepdims=True) + jnp.sum(en, axis=1, keepdims=True)
        o = (jnp.dot(ec.astype(BF16), vc, preferred_element_type=F32)
             + jnp.dot(en.astype(BF16), vn, preferred_element_type=F32)) / den
        lse = m + jnp.log(den)
        gr = G * R
        lg = [lse[g * gr:(g + 1) * gr] for g in range(N_DIL)]
        mm = jnp.maximum(jnp.maximum(lg[0], lg[1]), lg[2])
        eg = [jnp.exp(l - mm) for l in lg]
        inv = 1.0 / (eg[0] + eg[1] + eg[2])
        om = sum((eg[g] * inv) * o[g * gr:(g + 1) * gr] for g in range(N_DIL))
        for i in range(G):
            h = kvh * G + i
            o_ref[:, h * HD_C:(h + 1) * HD_C] = om[i * R:(i + 1) * R]


def _attn_c_sample(mix, proj, kv, cache_k, cache_v, layer, cosq, sinq, n_prompt, DB):
    r0 = n_prompt // SAMPLE_PAD
    lb = cache_k.shape[2]
    cspec = pl.BlockSpec((None, None, lb * KV_C, HD_C), lambda b: (layer, b, 0, 0))
    as_rows = lambda c: c.reshape(c.shape[0], DB, lb * KV_C, HD_C)
    return pl.pallas_call(
        functools.partial(_attn_c_sample_kernel, lb=lb),
        grid=(DB,),
        in_specs=[pl.BlockSpec((SAMPLE_PAD, ODD_IN), lambda b: (r0 + b, 0)),
                  pl.BlockSpec((SAMPLE_PAD, 2 * KVW_C), lambda b: (r0 + b, 0)),
                  cspec, cspec,
                  pl.BlockSpec((SAMPLE_PAD, LANES), lambda b: (0, 0)),
                  pl.BlockSpec((SAMPLE_PAD, LANES), lambda b: (0, 0)),
                  pl.BlockSpec(memory_space=pl.ANY)],
        out_specs=pl.BlockSpec((SAMPLE_PAD, QW_C), lambda b: (r0 + b, 0)),
        out_shape=jax.ShapeDtypeStruct(mix.shape, F32),
        input_output_aliases={6: 0},
        compiler_params=_cparams(("parallel",)), name="attn_c_sample",
    )(proj, kv, as_rows(cache_k), as_rows(cache_v), cosq, sinq, mix)


def _router_kernel(x_ref, rwh_ref, rwl_ref, rb_ref, idx_ref, gate_ref, rank_ref, cnt_ref, carry_ref):
    tm = x_ref.shape[0]

    @pl.when(pl.program_id(0) == 0)
    def _():
        carry_ref[...] = jnp.zeros_like(carry_ref)

    ninf = -jnp.inf
    x = x_ref[...]
    xh = x.astype(BF16)
    xl = (x - xh.astype(F32)).astype(BF16)
    wh = rwh_ref[...]
    logits = (jnp.dot(xh, wh, preferred_element_type=F32) + jnp.dot(xl, wh, preferred_element_type=F32)
              + jnp.dot(xh, rwl_ref[...], preferred_element_type=F32))
    pg = N_EXPERTS // N_EXPERT_GROUPS
    assert pg == 8 and TOP_K == 8 and N_EXPERT_GROUPS == 8
    scores = jax.nn.sigmoid(logits.T[:N_EXPERTS])
    choice = scores + jnp.concatenate([rb_ref[...]] * (tm // LANES), axis=1)
    sub = lax.broadcasted_iota(I32, (pg, tm), 0)
    smax = lambda v: jnp.max(v, axis=0, keepdims=True)
    smin = lambda v: jnp.min(v, axis=0, keepdims=True)
    ssum = lambda v: jnp.sum(v, axis=0, keepdims=True)
    cg = [choice[g * pg:(g + 1) * pg] for g in range(N_EXPERT_GROUPS)]
    sg = [scores[g * pg:(g + 1) * pg] for g in range(N_EXPERT_GROUPS)]
    gs = jnp.full((N_EXPERT_GROUPS, tm), ninf, F32)
    for g in range(N_EXPERT_GROUPS):
        m1 = smax(cg[g])
        i1 = smin(jnp.where(cg[g] == m1, sub, pg))
        m2 = smax(jnp.where(sub == i1, ninf, cg[g]))
        gs = jnp.where(sub == g, m1 + m2, gs)
    gmask = jnp.zeros((N_EXPERT_GROUPS, tm), jnp.bool_)
    for _ in range(TOPK_GROUPS):
        m = smax(gs)
        hit = sub == smin(jnp.where(gs == m, sub, N_EXPERT_GROUPS))
        gmask = gmask | hit
        gs = jnp.where(hit, ninf, gs)
    gsel = jnp.where(gmask, 1.0, 0.0)
    sel = [jnp.where(ssum(jnp.where(sub == g, gsel, 0.0)) > 0.5, cg[g], ninf) for g in range(N_EXPERT_GROUPS)]
    eid = [sub + g * pg for g in range(N_EXPERT_GROUPS)]
    tree = lambda op, xs: functools.reduce(op, xs)
    hots, idxs, gates = [], [], []
    for _ in range(TOP_K):
        m = smax(tree(jnp.maximum, sel))
        ik = smin(tree(jnp.minimum, [jnp.where(sel[g] == m, eid[g], N_EXPERTS) for g in range(N_EXPERT_GROUPS)]))
        hot = [eid[g] == ik for g in range(N_EXPERT_GROUPS)]
        gates.append(ssum(tree(jnp.add, [jnp.where(hot[g], sg[g], 0.0) for g in range(N_EXPERT_GROUPS)])))
        sel = [jnp.where(hot[g], ninf, sel[g]) for g in range(N_EXPERT_GROUPS)]
        hots.append(hot)
        idxs.append(ik)
    gsum = tree(jnp.add, gates)
    onehot = jnp.concatenate([tree(jnp.add, [hots[k][g].astype(F32) for k in range(TOP_K)])
                              for g in range(N_EXPERT_GROUPS)], axis=0)
    ri = lax.broadcasted_iota(I32, (tm, tm), 0)
    ci = lax.broadcasted_iota(I32, (tm, tm), 1)
    before = jnp.dot(onehot.astype(BF16), (ri < ci).astype(BF16), preferred_element_type=F32)
    base = before + jnp.concatenate([carry_ref[...]] * (tm // LANES), axis=1)
    idx_o = jnp.zeros((TOP_K, tm), I32)
    gate_o = jnp.zeros((TOP_K, tm), F32)
    rank_o = jnp.zeros((TOP_K, tm), I32)
    for k in range(TOP_K):
        rk = ssum(tree(jnp.add, [jnp.where(hots[k][g], base[g * pg:(g + 1) * pg], 0.0)
                                 for g in range(N_EXPERT_GROUPS)])).astype(I32)
        idx_o = jnp.where(sub == k, idxs[k], idx_o)
        gate_o = jnp.where(sub == k, gates[k] / gsum * ROUTED_SCALE, gate_o)
        rank_o = jnp.where(sub == k, rk, rank_o)
    idx_ref[...] = idx_o
    gate_ref[...] = gate_o
    rank_ref[...] = rank_o
    total = carry_ref[...] + jnp.sum(onehot, axis=1, keepdims=True)
    carry_ref[...] = total
    cnt_ref[...] = total


def _router(x, router_w, router_bias):
    T, D = x.shape
    tm = _pick(T, (256, 128))
    rw = jnp.concatenate([router_w, jnp.zeros((D, LANES - N_EXPERTS), F32)], axis=1)
    rwh = rw.astype(BF16)
    rwl = (rw - rwh.astype(F32)).astype(BF16)
    rb = jnp.broadcast_to(router_bias.astype(F32)[:, None], (N_EXPERTS, LANES))
    small = lambda: pl.BlockSpec((TOP_K, tm), lambda i: (0, i))
    return pl.pallas_call(
        _router_kernel, grid=(T // tm,),
        in_specs=[pl.BlockSpec((tm, D), lambda i: (i, 0)),
                  pl.BlockSpec((D, LANES), lambda i: (0, 0)),
                  pl.BlockSpec((D, LANES), lambda i: (0, 0)),
                  pl.BlockSpec((N_EXPERTS, LANES), lambda i: (0, 0))],
        out_specs=[small(), small(), small(), pl.BlockSpec((N_EXPERTS, LANES), lambda i: (0, 0))],
        out_shape=[jax.ShapeDtypeStruct((TOP_K, T), I32), jax.ShapeDtypeStruct((TOP_K, T), F32),
                   jax.ShapeDtypeStruct((TOP_K, T), I32), jax.ShapeDtypeStruct((N_EXPERTS, LANES), F32)],
        scratch_shapes=[pltpu.VMEM((N_EXPERTS, LANES), F32)],
        compiler_params=_cparams(("arbitrary",)), name="router",
    )(x, rwh, rwl, rb)


def _pack_bf16_pairs(x):
    half = x.shape[1] // 2
    lo = pltpu.bitcast(x[:, :half].astype(BF16).astype(F32), U32) >> 16
    hi = pltpu.bitcast(x[:, half:].astype(BF16).astype(F32), U32) & jnp.uint32(0xFFFF0000)
    return hi | lo


def _unpack_bf16_pairs(w):
    return pltpu.bitcast(w << 16, F32), pltpu.bitcast(w & jnp.uint32(0xFFFF0000), F32)


ROW_TILE = 8
OUT_TILE = 16


def _to_token_tiles(dst_ref, rows, n, rpt):
    for c in range(rpt):
        dst_ref[pl.ds(c, n, stride=rpt), :] = rows[:, c * LANES:(c + 1) * LANES]


def _from_token_tiles(load, n, rpt):
    return jnp.concatenate([load(pl.ds(c, n, stride=rpt)) for c in range(rpt)], axis=1)


def _tile_rows(slot, rpt=ROW_TILE):
    return pl.ds(pl.multiple_of(slot * rpt, rpt), rpt)


def _dispatch_kernel(zstart_ref, zcount_ref, x_ref, pos_hbm, xs_hbm, tiles_ref, ztile_ref, pos_ref, psem, sem):
    i = pl.program_id(0)
    tm = x_ref.shape[0]
    pcopy = pltpu.make_async_copy(pos_hbm.at[i], pos_ref, psem)
    pcopy.start()
    _to_token_tiles(tiles_ref, _pack_bf16_pairs(x_ref[...]), tm, ROW_TILE)

    def tile_copy(src, r, slot):
        return pltpu.make_async_copy(src.at[_tile_rows(r)], xs_hbm.at[_tile_rows(slot)], sem)

    @pl.when(i == 0)
    def _():
        ztile_ref[...] = jnp.zeros_like(ztile_ref)

        def per_expert(e, carry):
            def start(r, c):
                tile_copy(ztile_ref, 0, zstart_ref[e] + r).start()
                return c

            def wait(r, c):
                tile_copy(ztile_ref, 0, 0).wait()
                return c

            lax.fori_loop(0, zcount_ref[e], start, 0)
            lax.fori_loop(0, zcount_ref[e], wait, 0)
            return carry

        lax.fori_loop(0, N_EXPERTS, per_expert, 0)

    pcopy.wait()

    def start(r, c):
        for k in range(TOP_K):
            tile_copy(tiles_ref, r, pos_ref[k * tm + r]).start()
        return c

    def wait(r, c):
        for k in range(TOP_K):
            tile_copy(tiles_ref, 0, 0).wait()
        return c

    lax.fori_loop(0, tm, start, 0)
    lax.fori_loop(0, tm, wait, 0)


def _tile_major(a, tm):
    K, T = a.shape
    return a.reshape(K, T // tm, tm).transpose(1, 0, 2).reshape(T // tm, K * tm)


def _dispatch(x, pos, zstart, zcount, n_slots):
    T, D = x.shape
    assert D // 2 == ROW_TILE * LANES
    tm = _pick(T, (256, 128))
    return pl.pallas_call(
        _dispatch_kernel,
        grid_spec=pltpu.PrefetchScalarGridSpec(
            num_scalar_prefetch=2, grid=(T // tm,),
            in_specs=[pl.BlockSpec((tm, D), lambda i, *_: (i, 0)),
                      pl.BlockSpec(memory_space=pl.ANY)],
            out_specs=pl.BlockSpec(memory_space=pl.ANY),
            scratch_shapes=[pltpu.VMEM((tm * ROW_TILE, LANES), U32), pltpu.VMEM((ROW_TILE, LANES), U32),
                            pltpu.SMEM((tm * TOP_K,), I32),
                            pltpu.SemaphoreType.DMA, pltpu.SemaphoreType.DMA]),
        out_shape=jax.ShapeDtypeStruct((n_slots * ROW_TILE, LANES), U32),
        compiler_params=_cparams(("arbitrary",)),
        name="moe_dispatch",
    )(zstart, zcount, x, _tile_major(pos, tm))


def _experts_kernel(te_ref, nu_ref, ne_ref, sl_ref, xs_ref, w1_hbm, w3_hbm, w2_hbm, ys_ref,
                    wf1, wf3, wf2, w1b, w3b, w2b, sems, *, layer):
    i = pl.program_id(0)
    te = EXPERT_TILE

    def weight_copies(e, s):
        return (pltpu.make_async_copy(w1_hbm.at[layer, e], wf1.at[s], sems.at[s, 0]),
                pltpu.make_async_copy(w3_hbm.at[layer, e], wf3.at[s], sems.at[s, 1]),
                pltpu.make_async_copy(w2_hbm.at[layer, e], wf2.at[s], sems.at[s, 2]))

    @pl.when(i < nu_ref[0])
    def _():
        e, s = te_ref[i], sl_ref[i]

        @pl.when(i == 0)
        def _():
            for cp in weight_copies(e, s):
                cp.start()

        @pl.when((i == 0) | (e != te_ref[jnp.maximum(i - 1, 0)]))
        def _():
            for cp in weight_copies(e, s):
                cp.wait()
            w1b[...] = wf1[s].astype(BF16)
            w3b[...] = wf3[s].astype(BF16)
            w2b[...] = wf2[s].astype(BF16)

            @pl.when(ne_ref[i] >= 0)
            def _():
                for cp in weight_copies(ne_ref[i], 1 - s):
                    cp.start()

        lo, hi = _unpack_bf16_pairs(_from_token_tiles(lambda rows: xs_ref[rows, :], te, ROW_TILE))
        lo, hi = lo.astype(BF16), hi.astype(BF16)
        half = lo.shape[1]
        h1 = (jnp.dot(lo, w1b[:half, :], preferred_element_type=F32)
              + jnp.dot(hi, w1b[half:, :], preferred_element_type=F32))
        h3 = (jnp.dot(lo, w3b[:half, :], preferred_element_type=F32)
              + jnp.dot(hi, w3b[half:, :], preferred_element_type=F32))
        y = jnp.dot((_silu(h1) * h3).astype(BF16), w2b[...], preferred_element_type=F32)
        ys_ref[...] = y


def _experts(xs, tile_e, n_used, next_e, slot, w1, w3, w2, layer):
    te = EXPERT_TILE
    n_slots = xs.shape[0] // ROW_TILE
    nt = n_slots // te
    D, DE = w1.shape[2], w1.shape[3]
    assert D == OUT_TILE * LANES
    live = lambda i, te_, nu, *_: (jnp.minimum(i, nu[0] - 1), 0)
    anyspec = pl.BlockSpec(memory_space=pl.ANY)
    return pl.pallas_call(
        functools.partial(_experts_kernel, layer=layer),
        grid_spec=pltpu.PrefetchScalarGridSpec(
            num_scalar_prefetch=4, grid=(nt,),
            in_specs=[pl.BlockSpec((te * ROW_TILE, LANES), live), anyspec, anyspec, anyspec],
            out_specs=pl.BlockSpec((te, D), live),
            scratch_shapes=[pltpu.VMEM((2, D, DE), F32), pltpu.VMEM((2, D, DE), F32), pltpu.VMEM((2, DE, D), F32),
                            pltpu.VMEM((D, DE), BF16), pltpu.VMEM((D, DE), BF16), pltpu.VMEM((DE, D), BF16),
                            pltpu.SemaphoreType.DMA((2, 3))]),
        out_shape=jax.ShapeDtypeStruct((n_slots, D), F32),
        compiler_params=_cparams(("arbitrary",)), name="moe_experts",
    )(tile_e, n_used, next_e, slot, xs, w1, w3, w2)


def _shared_kernel(x_ref, w1_ref, w3_ref, w2_ref, o_ref, w1b, w3b, w2b):
    @pl.when(pl.program_id(0) == 0)
    def _():
        w1b[...] = w1_ref[...].astype(BF16)
        w3b[...] = w3_ref[...].astype(BF16)
        w2b[...] = w2_ref[...].astype(BF16)

    xb = x_ref[...].astype(BF16)
    h1 = jnp.dot(xb, w1b[...], preferred_element_type=F32)
    h3 = jnp.dot(xb, w3b[...], preferred_element_type=F32)
    o_ref[...] = jnp.dot((_silu(h1) * h3).astype(BF16), w2b[...], preferred_element_type=F32)


def _shared_expert(x, w1, w3, w2, layer):
    T, D = x.shape
    DE = w1.shape[2]
    tm = _pick(T, (256, 128))
    const = lambda r, c: pl.BlockSpec((None, r, c), lambda i: (layer, 0, 0))
    return pl.pallas_call(
        _shared_kernel, grid=(T // tm,),
        in_specs=[pl.BlockSpec((tm, D), lambda i: (i, 0)), const(D, DE), const(D, DE), const(DE, D)],
        out_specs=pl.BlockSpec((tm, D), lambda i: (i, 0)),
        out_shape=jax.ShapeDtypeStruct((T, D), F32),
        scratch_shapes=[pltpu.VMEM((D, DE), BF16), pltpu.VMEM((D, DE), BF16), pltpu.VMEM((DE, D), BF16)],
        compiler_params=_cparams(("arbitrary",)), name="moe_shared",
    )(x, w1, w3, w2)


def _combine_kernel(x_ref, sh_ref, gate_ref, g_ref, b_ref, pos_hbm, ys_hbm, o_ref, buf_ref, pos_ref, psem, sem):
    i = pl.program_id(0)
    tm = x_ref.shape[0]
    pcopy = pltpu.make_async_copy(pos_hbm.at[i], pos_ref, psem)
    pcopy.start()
    pcopy.wait()

    def tile_copy(slot, k, r):
        return pltpu.make_async_copy(ys_hbm.at[pl.ds(slot, 1)], buf_ref.at[k, pl.ds(r, 1)], sem)

    def start(r, c):
        for k in range(TOP_K):
            tile_copy(pos_ref[k * tm + r], k, r).start()
        return c

    def wait(r, c):
        for k in range(TOP_K):
            tile_copy(0, 0, 0).wait()
        return c

    lax.fori_loop(0, tm, start, 0)
    lax.fori_loop(0, tm, wait, 0)
    gate = gate_ref[...]
    y = sh_ref[...]
    for k in range(TOP_K):
        y = y + gate[:, k:k + 1] * buf_ref[k]
    o_ref[...] = _layernorm_rows(ALPHA * x_ref[...] + y, g_ref[...], b_ref[...])


def _combine(x, shared, gate, pos, ys, g, b):
    T, D = x.shape
    tm = BLK
    row = pl.BlockSpec((tm, D), lambda i: (i, 0))
    vec = pl.BlockSpec((1, D), lambda i: (0, 0))
    return pl.pallas_call(
        _combine_kernel, grid=(T // tm,),
        in_specs=[row, row, pl.BlockSpec((tm, TOP_K), lambda i: (i, 0)), vec, vec,
                  pl.BlockSpec(memory_space=pl.ANY), pl.BlockSpec(memory_space=pl.ANY)],
        out_specs=row,
        out_shape=jax.ShapeDtypeStruct((T, D), F32),
        scratch_shapes=[pltpu.VMEM((TOP_K, tm, D), F32), pltpu.SMEM((tm * TOP_K,), I32),
                        pltpu.SemaphoreType.DMA, pltpu.SemaphoreType.DMA],
        compiler_params=_cparams(("arbitrary",)), name="moe_combine",
    )(x, shared, gate, g.reshape(1, D), b.reshape(1, D), _tile_major(pos, tm), ys)


def _moe_ln(x, g, b, router_w, router_bias, w1, w3, w2, w1s, w3s, w2s, layer):
    T = x.shape[0]
    idx, gate, rank, cnt = _router(x, router_w, router_bias)
    te = EXPERT_TILE
    counts = cnt[:, 0].astype(I32)
    padded = (counts + te - 1) // te * te
    pad_end = jnp.cumsum(padded)
    pad_start = pad_end - padded
    pos = pad_start[idx] + rank
    nt = (T * TOP_K + N_EXPERTS * (te - 1)) // te + 1
    n_used = (pad_end[-1] // te).astype(I32)
    tiles = jnp.minimum(jnp.arange(nt, dtype=I32), n_used - 1)
    tile_e = jnp.minimum(jnp.sum((pad_end[None, :] <= (tiles * te)[:, None]).astype(I32), axis=1), N_EXPERTS - 1)
    eids = jnp.arange(N_EXPERTS, dtype=I32)
    used = padded > 0
    later_used = used[None, :] & (eids[None, :] > eids[:, None])
    next_used = jnp.min(jnp.where(later_used, eids[None, :], N_EXPERTS), axis=1)
    next_used = jnp.where(next_used < N_EXPERTS, next_used, -1).astype(I32)
    run_index = jnp.cumsum(used.astype(I32)) - 1
    xs = _dispatch(x, pos, (pad_start + counts).astype(I32), (padded - counts).astype(I32), nt * te)
    ys = _experts(xs, tile_e, n_used.reshape(1), next_used[tile_e], (run_index[tile_e] % 2).astype(I32),
                  w1, w3, w2, layer)
    shared = _shared_expert(x, w1s, w3s, w2s, layer)
    return _combine(x, shared, gate.T, pos, ys, g, b)


def _ple_kernel(x_ref, xr_ref, p_ref, wg_ref, wp_ref, o_ref, xb_ref, pb_ref):
    @pl.when(pl.program_id(1) == 0)
    def _():
        xb_ref[...] = x_ref[...].astype(BF16)
        pb_ref[...] = p_ref[...].astype(BF16)

    gate = jax.nn.sigmoid(jnp.dot(xb_ref[...], wg_ref[...].astype(BF16), preferred_element_type=F32))
    proj = jnp.dot(pb_ref[...], wp_ref[...].astype(BF16), preferred_element_type=F32)
    o_ref[...] = xr_ref[...] + gate * proj


def _ple(x, p, wg, wp, layer, tn=512):
    T, D = x.shape
    P = p.shape[1]
    tm = _pick(T, (768, 512, 384, 256, 128))
    return pl.pallas_call(
        _ple_kernel, grid=(T // tm, D // tn),
        in_specs=[pl.BlockSpec((tm, D), lambda i, j: (i, 0)),
                  pl.BlockSpec((tm, tn), lambda i, j: (i, j)),
                  pl.BlockSpec((tm, P), lambda i, j: (i, 0)),
                  pl.BlockSpec((None, D, tn), lambda i, j: (layer, 0, j)),
                  pl.BlockSpec((None, P, tn), lambda i, j: (layer, 0, j))],
        out_specs=pl.BlockSpec((tm, tn), lambda i, j: (i, j)),
        out_shape=jax.ShapeDtypeStruct((T, D), F32),
        scratch_shapes=[pltpu.VMEM((tm, D), BF16), pltpu.VMEM((tm, P), BF16)],
        compiler_params=_cparams(("parallel", "arbitrary")), name="ple",
    )(x, x, p, wg, wp)


def _pad_time(a, axis=1):
    pad = [(0, 0)] * a.ndim
    pad[axis] = (0, SAMPLE_PAD - a.shape[axis])
    return jnp.pad(a, pad)


def kernel(x_prompt, x_sample, cache_a_k, cache_a_v, state_b_ssm, state_b_conv, cache_c_k, cache_c_v, p_prompt, p_sample, w_in_even, sink_a, conv_w_b, conv_b_b, dt_bias_b, a_log_b, d_skip_b, norm_w_b, w_out_even, w_in_odd, w_out_odd, ln_g, ln_b, router_w, router_bias, w1_e, w3_e, w2_e, w1_s, w3_s, w2_s, w_ple_gate, w_ple_proj):
    B, L, D = x_prompt.shape
    DB, TS, _ = x_sample.shape
    NP = B * L
    assert L % (BLK * max(d for _, d in DILATIONS)) == 0 and TS <= SAMPLE_PAD and CONV_K - 1 <= TS
    NS = DB * SAMPLE_PAD
    npb = NP // BLK

    x = jnp.concatenate([x_prompt.reshape(NP, D), _pad_time(x_sample).reshape(NS, D)], axis=0)

    prompt_pos = np.arange(L)
    sample_pos = PAST_LEN + np.arange(SAMPLE_PAD)
    all_pos = np.concatenate([prompt_pos, np.tile(sample_pos, BLK // SAMPLE_PAD)])
    tabs = {}
    for name, hd in (("a", HD_A), ("c", HD_C)):
        sc = hd ** -0.5
        tabs[name] = dict(k=_rope_tables(all_pos, hd), qp=_rope_tables(prompt_pos, hd, sc),
                          qs=_rope_tables(sample_pos, hd, sc))

    zeros_carry = jnp.zeros((B, 8, CONV_DIM), F32)
    zeros_state = jnp.zeros((B, D_INNER, D_STATE), F32)
    ssm_in = state_b_ssm.reshape(-1, D_INNER, D_STATE)

    outs = {k: [] for k in ("pa_k", "pa_v", "pb_ssm", "pb_conv", "pc_k", "pc_v",
                            "sa_k", "sa_v", "sb_ssm", "sb_conv", "sc_k", "sc_v")}
    for i in range(DEPTH):
        j = i // 2
        if i % 2 == 0:
            proj = _matmul(x, w_in_even, j)
            kv = _rope_kv(proj, *tabs["a"]["k"], npb, L // BLK, HD_A, KV_A * HD_A, OFF_K_A)
            mix = _attn_a_prompt(proj, kv, sink_a[j], *tabs["a"]["qp"], B, L)
            mix = _attn_a_sample(mix, proj, kv, cache_a_k[j], cache_a_v[j], sink_a[j], *tabs["a"]["qs"], NP, DB)
            ssm_w = (conv_w_b[j], conv_b_b[j], dt_bias_b[j], a_log_b[j], d_skip_b[j], norm_w_b[j])
            mix, hp = _ssd(mix, proj, zeros_carry, zeros_state, *ssm_w,
                           row0=0, nbatch=B, nc=L // BLK, rows=BLK, valid=BLK)
            carry_s = jnp.pad(state_b_conv[j], ((0, 0), (8 - (CONV_K - 1), 0), (0, 0)))
            mix, hs = _ssd(mix, proj, carry_s, ssm_in, *ssm_w,
                           row0=NP, nbatch=DB, nc=1, rows=SAMPLE_PAD, valid=TS, h0_base=j * DB)
            w_out = w_out_even
            kvp = kv[:NP].reshape(B, L, 2, KV_A, HD_A)
            kvs = kv[NP:].reshape(DB, SAMPLE_PAD, 2, KV_A, HD_A)
            keep = min(WINDOW_A, L)
            outs["pa_k"].append(kvp[:, L - keep:, 0]); outs["pa_v"].append(kvp[:, L - keep:, 1])
            outs["sa_k"].append(kvs[:, :TS, 0]); outs["sa_v"].append(kvs[:, :TS, 1])
            outs["pb_ssm"].append(hp.reshape(B, H_B, HD_B, D_STATE))
            outs["sb_ssm"].append(hs.reshape(DB, H_B, HD_B, D_STATE))
            outs["pb_conv"].append(proj[:NP].reshape(B, L, EVEN_IN)[:, L - (CONV_K - 1):, OFF_XBC:OFF_DT])
            outs["sb_conv"].append(proj[NP:].reshape(DB, SAMPLE_PAD, EVEN_IN)[:, TS - (CONV_K - 1):TS, OFF_XBC:OFF_DT])
        else:
            proj = _matmul(x, w_in_odd, j)
            kv, k_heads, v_heads = _rope_kv(proj, *tabs["c"]["k"], npb, L // BLK, HD_C, KVW_C, OFF_K_C, per_head=True)
            go, gl = [], []
            for g, (win, dil) in enumerate(DILATIONS):
                assert win // dil == BLK
                if dil == 1:
                    o_g, l_g = _attn_c_prompt(proj, kv, *tabs["c"]["qp"], g, B, L)
                else:
                    o_g, l_g = _attn_c_dilated(proj, kv, *tabs["c"]["qp"], g, dil, B, L)
                go.append(o_g); gl.append(l_g)
            mix = _mix_c(go, gl, NP)
            mix = _attn_c_sample(mix, proj, kv, cache_c_k, cache_c_v, j, *tabs["c"]["qs"], NP, DB)
            w_out = w_out_odd
            keep = min(max(w for w, _ in DILATIONS), L)
            for name, heads in (("k", k_heads), ("v", v_heads)):
                outs["pc_" + name].append(heads[:NP * KV_C].reshape(B, L, KV_C, HD_C)[:, L - keep:])
                outs["sc_" + name].append(heads[NP * KV_C:].reshape(DB, SAMPLE_PAD, KV_C, HD_C)[:, :TS])
        x = _matmul_add_ln(mix, w_out, j, x, ln_g[i, 0], ln_b[i, 0])
        x = _moe_ln(x, ln_g[i, 1], ln_b[i, 1], router_w[i], router_bias[i], w1_e, w3_e, w2_e,
                    w1_s, w3_s, w2_s, i)
        p = jnp.concatenate([p_prompt[i].reshape(NP, PLE_DIM), _pad_time(p_sample[i]).reshape(NS, PLE_DIM)], axis=0)
        x = _ple(x, p, w_ple_gate, w_ple_proj, i)

    st = lambda k: jnp.stack(outs[k])
    y_prompt = x[:NP].reshape(B, L, D)
    y_sample = x[NP:].reshape(DB, SAMPLE_PAD, D)[:, :TS]
    return (y_prompt, y_sample, st("pa_k"), st("pa_v"), st("pb_ssm"), st("pb_conv"), st("pc_k"), st("pc_v"),
            st("sa_k"), st("sa_v"), st("sb_ssm"), st("sb_conv"), st("sc_k"), st("sc_v"))
```

```python
import functools
import math

import numpy as np
import jax
import jax.numpy as jnp
from jax import lax
from jax.experimental import pallas as pl
from jax.experimental.pallas import tpu as pltpu

F32 = jnp.float32
BF16 = jnp.bfloat16
I32 = jnp.int32
U32 = jnp.uint32

D_MODEL = 2048
DEPTH = 4
PAST_LEN = 16384
ALPHA = (2.0 * DEPTH) ** 0.25
LN_EPS = 1e-5
RMS_EPS = 1e-5
ROPE_THETA = 10000.0
BLK = 128

HD_A = 64
H_A = D_MODEL // (2 * HD_A)
KV_A = H_A // 8
WINDOW_A = 128

HD_B = 64
D_INNER = D_MODEL // 2
H_B = D_INNER // HD_B
NG_B = 2
D_STATE = 128
CONV_K = 4
CONV_DIM = D_INNER + 2 * NG_B * D_STATE
EVEN_IN = H_A * HD_A + 2 * KV_A * HD_A + D_INNER + CONV_DIM + H_B
OFF_K_A = H_A * HD_A
OFF_V_A = OFF_K_A + KV_A * HD_A
OFF_Z = OFF_V_A + KV_A * HD_A
OFF_XBC = OFF_Z + D_INNER
OFF_DT = OFF_XBC + CONV_DIM

HD_C = 128
H_C = D_MODEL // HD_C
KV_C = 4
DILATIONS = ((128, 1), (512, 4), (2048, 16))
N_DIL = len(DILATIONS)
QW_C = H_C * HD_C
KVW_C = KV_C * HD_C
ODD_IN = N_DIL * QW_C + 2 * KVW_C
OFF_K_C = N_DIL * QW_C
OFF_V_C = OFF_K_C + KVW_C

N_EXPERTS = 64
N_EXPERT_GROUPS = 8
TOPK_GROUPS = 4
TOP_K = 8
D_EXPERT = D_MODEL // 4
ROUTED_SCALE = 2.5
PLE_DIM = 256

SAMPLE_PAD = 8
LANES = 128
VMEM_LIMIT = 56 * 1024 * 1024
EXPERT_TILE = 256
NEG = -1e30


def _cparams(sem, vmem=None):
    return pltpu.CompilerParams(dimension_semantics=sem, vmem_limit_bytes=vmem or VMEM_LIMIT)


def _pick(n, cands):
    for c in cands:
        if n % c == 0:
            return c
    raise ValueError(f"no tile for {n}")


def _mm_kernel(x_ref, w_ref, o_ref, xb_ref):
    @pl.when(pl.program_id(1) == 0)
    def _():
        xb_ref[...] = x_ref[...].astype(BF16)

    o_ref[...] = jnp.dot(xb_ref[...], w_ref[...].astype(BF16), preferred_element_type=F32)


def _matmul(x, w, layer, tn=512):
    M, K = x.shape
    N = w.shape[2]
    tm = _pick(M, (768, 512, 384, 256, 128))
    return pl.pallas_call(
        _mm_kernel,
        grid=(M // tm, pl.cdiv(N, tn)),
        in_specs=[pl.BlockSpec((tm, K), lambda i, j: (i, 0)),
                  pl.BlockSpec((None, K, tn), lambda i, j: (layer, 0, j))],
        out_specs=pl.BlockSpec((tm, tn), lambda i, j: (i, j)),
        out_shape=jax.ShapeDtypeStruct((M, N), F32),
        scratch_shapes=[pltpu.VMEM((tm, K), BF16)],
        compiler_params=_cparams(("parallel", "arbitrary")),
        name="matmul",
    )(x, w)


def _layernorm_rows(v, g, b):
    mu = jnp.mean(v, axis=-1, keepdims=True)
    vc = v - mu
    var = jnp.mean(vc * vc, axis=-1, keepdims=True)
    return vc * lax.rsqrt(var + LN_EPS) * g + b


def _mm_ln_kernel(a_ref, w_ref, x_ref, g_ref, b_ref, o_ref, acc_ref):
    k = pl.program_id(1)

    @pl.when(k == 0)
    def _():
        acc_ref[...] = jnp.zeros_like(acc_ref)

    acc_ref[...] += jnp.dot(a_ref[...].astype(BF16), w_ref[...].astype(BF16), preferred_element_type=F32)

    @pl.when(k == pl.num_programs(1) - 1)
    def _():
        o_ref[...] = _layernorm_rows(ALPHA * x_ref[...] + acc_ref[...], g_ref[...], b_ref[...])


def _matmul_add_ln(a, w, layer, x, g, b, tk=512):
    T, K = a.shape
    D = w.shape[2]
    tm = _pick(T, (768, 512, 384, 256, 128))
    row = pl.BlockSpec((tm, D), lambda i, k: (i, 0))
    vec = pl.BlockSpec((1, D), lambda i, k: (0, 0))
    return pl.pallas_call(
        _mm_ln_kernel, grid=(T // tm, K // tk),
        in_specs=[pl.BlockSpec((tm, tk), lambda i, k: (i, k)),
                  pl.BlockSpec((None, tk, D), lambda i, k: (layer, k, 0)),
                  row, vec, vec],
        out_specs=row,
        out_shape=jax.ShapeDtypeStruct((T, D), F32),
        scratch_shapes=[pltpu.VMEM((tm, D), F32)],
        compiler_params=_cparams(("parallel", "arbitrary")), name="matmul_add_ln",
    )(a, w, x, g.reshape(1, D), b.reshape(1, D))


def _rope_tables(pos, head_dim, scale=1.0):
    half = head_dim // 2
    inv = ROPE_THETA ** (-np.arange(half, dtype=np.float64) / half)
    ang = np.asarray(pos, np.float64)[:, None] * inv
    cos, sin = np.cos(ang), np.sin(ang)
    reps = LANES // head_dim
    cos_t = np.tile(np.concatenate([cos, cos], -1), (1, reps)) * scale
    sin_t = np.tile(np.concatenate([-sin, sin], -1), (1, reps)) * scale
    return jnp.asarray(cos_t, F32), jnp.asarray(sin_t, F32)


def _rope_chunk(x, cos, sin, head_dim):
    half = head_dim // 2
    if head_dim == LANES:
        rot = pltpu.roll(x, half, axis=1)
    else:
        lane = lax.broadcasted_iota(I32, x.shape, 1)
        first = (lane & (head_dim - 1)) < half
        rot = jnp.where(first, pltpu.roll(x, LANES - half, axis=1), pltpu.roll(x, half, axis=1))
    return x * cos + rot * sin


def _rope_kv_kernel(k_ref, v_ref, cos_ref, sin_ref, o_ref, *head_refs, head_dim, width):
    cos, sin = cos_ref[...], sin_ref[...]
    for c in range(width // LANES):
        sl = slice(c * LANES, (c + 1) * LANES)
        kr = _rope_chunk(k_ref[:, sl], cos, sin, head_dim)
        o_ref[:, sl] = kr
        if head_refs:
            nh = width // LANES
            head_refs[0][pl.ds(c, BLK, stride=nh), :] = kr
            head_refs[1][pl.ds(c, BLK, stride=nh), :] = v_ref[:, sl]
    o_ref[:, width:] = v_ref[...]


def _rope_kv(proj, cos_all, sin_all, n_prompt_blocks, pos_blocks, head_dim, width, k_off, per_head=False):
    T = proj.shape[0]
    kb = k_off // width
    tab = lambda i: (jnp.where(i < n_prompt_blocks, i % pos_blocks, pos_blocks), 0)
    out_specs = [pl.BlockSpec((BLK, 2 * width), lambda i: (i, 0))]
    out_shape = [jax.ShapeDtypeStruct((T, 2 * width), F32)]
    if per_head:
        assert head_dim == LANES
        nh = width // LANES
        out_specs += [pl.BlockSpec((BLK * nh, LANES), lambda i: (i, 0))] * 2
        out_shape += [jax.ShapeDtypeStruct((T * nh, LANES), F32)] * 2
    res = pl.pallas_call(
        functools.partial(_rope_kv_kernel, head_dim=head_dim, width=width),
        grid=(T // BLK,),
        in_specs=[pl.BlockSpec((BLK, width), lambda i: (i, kb)),
                  pl.BlockSpec((BLK, width), lambda i: (i, kb + 1)),
                  pl.BlockSpec((BLK, LANES), tab), pl.BlockSpec((BLK, LANES), tab)],
        out_specs=out_specs, out_shape=out_shape,
        compiler_params=_cparams(("parallel",)), name="rope_kv",
    )(proj, proj, cos_all, sin_all)
    return res if per_head else res[0]


def _softmax_pv(s, vv, sink_col=None):
    m = jnp.max(s, axis=1, keepdims=True)
    if sink_col is not None:
        m = jnp.maximum(m, sink_col)
    e = jnp.exp(s - m)
    den = jnp.sum(e, axis=1, keepdims=True)
    if sink_col is not None:
        den = den + jnp.exp(sink_col - m)
    o = jnp.dot(e.astype(BF16), vv, preferred_element_type=F32) / den
    return o, m + jnp.log(den)


def _qk(qs, kk):
    return lax.dot_general(qs, kk, (((1,), (1,)), ((), ())), preferred_element_type=F32)


def _band_mask(rows, first_block):
    qi = lax.broadcasted_iota(I32, (rows, 2 * BLK), 0) & (BLK - 1)
    kj = lax.broadcasted_iota(I32, (rows, 2 * BLK), 1)
    ok = (kj >= qi) & (kj <= qi + BLK)
    return ok & ((kj >= BLK) | jnp.logical_not(first_block))


def _dup_half(x, which):
    lane = lax.broadcasted_iota(I32, x.shape, 1)
    sw = pltpu.roll(x, HD_A, axis=1)
    return jnp.where(lane < HD_A, x, sw) if which == 0 else jnp.where(lane < HD_A, sw, x)


def _attn_a_heads(q_chunks, kdup, vdup, mask, sink_ref, kvh):
    R = q_chunks[0].shape[0]
    lane = lax.broadcasted_iota(I32, (R, LANES), 1)
    lo = lane < HD_A
    parts, sinks = [], []
    for c, ch in enumerate(q_chunks):
        parts.append(jnp.where(lo, ch, 0.0))
        parts.append(jnp.where(lo, 0.0, ch))
        for par in range(2):
            sinks.append(jnp.full((R, 1), sink_ref[kvh * 8 + 2 * c + par], F32))
    qs = jnp.concatenate(parts, axis=0).astype(BF16)
    s = jnp.where(mask, _qk(qs, kdup), NEG)
    o, _ = _softmax_pv(s, vdup, jnp.concatenate(sinks, axis=0))
    return [jnp.where(lo, o[(2 * c) * R:(2 * c + 1) * R], o[(2 * c + 1) * R:(2 * c + 2) * R])
            for c in range(len(q_chunks))]


def _attn_a_prompt_kernel(sink_ref, q_ref, kvp_ref, kvc_ref, cos_ref, sin_ref, o_ref):
    first = pl.program_id(1) == 0
    cos, sin = cos_ref[...], sin_ref[...]
    kvp, kvc = kvp_ref[...], kvc_ref[...]
    kk = jnp.concatenate([kvp[:, :LANES], kvc[:, :LANES]], axis=0)
    vv = jnp.concatenate([kvp[:, LANES:], kvc[:, LANES:]], axis=0)
    mask = _band_mask(8 * BLK, first)
    for kvh in range(KV_A):
        kdup = _dup_half(kk, kvh).astype(BF16)
        vdup = _dup_half(vv, kvh).astype(BF16)
        chunks = [_rope_chunk(q_ref[:, (kvh * 4 + c) * LANES:(kvh * 4 + c + 1) * LANES], cos, sin, HD_A)
                  for c in range(4)]
        outs = _attn_a_heads(chunks, kdup, vdup, mask, sink_ref, kvh)
        for c in range(4):
            o_ref[:, (kvh * 4 + c) * LANES:(kvh * 4 + c + 1) * LANES] = outs[c]


def _attn_a_prompt(proj, kv, sink, cosq, sinq, B, L):
    T = proj.shape[0]
    nb = L // BLK
    return pl.pallas_call(
        _attn_a_prompt_kernel,
        grid=(B, nb),
        in_specs=[pl.BlockSpec(memory_space=pltpu.SMEM),
                  pl.BlockSpec((BLK, OFF_K_A), lambda b, m: (b * nb + m, 0)),
                  pl.BlockSpec((BLK, 2 * LANES), lambda b, m: (b * nb + jnp.maximum(m - 1, 0), 0)),
                  pl.BlockSpec((BLK, 2 * LANES), lambda b, m: (b * nb + m, 0)),
                  pl.BlockSpec((BLK, LANES), lambda b, m: (m, 0)),
                  pl.BlockSpec((BLK, LANES), lambda b, m: (m, 0))],
        out_specs=pl.BlockSpec((BLK, OFF_K_A), lambda b, m: (b * nb + m, 0)),
        out_shape=jax.ShapeDtypeStruct((T, D_MODEL), F32),
        compiler_params=_cparams(("parallel", "parallel")), name="attn_a_prompt",
    )(sink, proj, kv, kv, cosq, sinq)


def _attn_a_sample_kernel(sink_ref, q_ref, kvn_ref, ck_ref, cv_ref, cos_ref, sin_ref, mix_ref, o_ref):
    del mix_ref
    R = SAMPLE_PAD
    cos, sin = cos_ref[...], sin_ref[...]
    kvn = kvn_ref[...]
    zpad = jnp.zeros((BLK - R, LANES), F32)
    kk = jnp.concatenate([ck_ref[...], kvn[:, :LANES], zpad], axis=0)
    vv = jnp.concatenate([cv_ref[...], kvn[:, LANES:], zpad], axis=0)
    t = lax.broadcasted_iota(I32, (8 * R, 2 * BLK), 0) & (R - 1)
    j = lax.broadcasted_iota(I32, (8 * R, 2 * BLK), 1)
    mask = ((j < BLK) & (j >= t)) | ((j >= BLK) & (j - BLK <= t) & (j < BLK + R))
    for kvh in range(KV_A):
        kdup = _dup_half(kk, kvh).astype(BF16)
        vdup = _dup_half(vv, kvh).astype(BF16)
        chunks = [_rope_chunk(q_ref[:, (kvh * 4 + c) * LANES:(kvh * 4 + c + 1) * LANES], cos, sin, HD_A)
                  for c in range(4)]
        outs = _attn_a_heads(chunks, kdup, vdup, mask, sink_ref, kvh)
        for c in range(4):
            o_ref[:, (kvh * 4 + c) * LANES:(kvh * 4 + c + 1) * LANES] = outs[c]


def _attn_a_sample(mix, proj, kv, cache_k, cache_v, sink, cosq, sinq, n_prompt, DB):
    r0 = n_prompt // SAMPLE_PAD
    lb = cache_k.shape[1]
    assert lb == BLK
    return pl.pallas_call(
        _attn_a_sample_kernel,
        grid=(DB,),
        in_specs=[pl.BlockSpec(memory_space=pltpu.SMEM),
                  pl.BlockSpec((SAMPLE_PAD, OFF_K_A), lambda b: (r0 + b, 0)),
                  pl.BlockSpec((SAMPLE_PAD, 2 * LANES), lambda b: (r0 + b, 0)),
                  pl.BlockSpec((None, lb, LANES), lambda b: (b, 0, 0)),
                  pl.BlockSpec((None, lb, LANES), lambda b: (b, 0, 0)),
                  pl.BlockSpec((SAMPLE_PAD, LANES), lambda b: (0, 0)),
                  pl.BlockSpec((SAMPLE_PAD, LANES), lambda b: (0, 0)),
                  pl.BlockSpec(memory_space=pl.ANY)],
        out_specs=pl.BlockSpec((SAMPLE_PAD, OFF_K_A), lambda b: (r0 + b, 0)),
        out_shape=jax.ShapeDtypeStruct(mix.shape, F32),
        input_output_aliases={7: 0},
        compiler_params=_cparams(("parallel",)), name="attn_a_sample",
    )(sink, proj, kv, cache_k.reshape(DB, lb, LANES), cache_v.reshape(DB, lb, LANES), cosq, sinq, mix)


def _hi_dot(a, b):
    return jnp.dot(a, b, precision=lax.Precision.HIGHEST, preferred_element_type=F32)


def _silu(x):
    return x * jax.nn.sigmoid(x)


def _ssd_kernel(proj_ref, carry0_ref, h0_ref, cw_ref, cb_ref, dtb_ref, alog_ref, dskip_ref, nw_ref,
                exp_ref, mix_ref, y_ref, hout_ref, carry_ref, state_ref, *, rows, valid, nc):
    del mix_ref
    c = pl.program_id(1)
    Q = BLK

    @pl.when(c == 0)
    def _():
        carry_ref[...] = carry0_ref[...]
        state_ref[...] = h0_ref[...].T

    blk = proj_ref[...]
    if rows < Q:
        blk = jnp.concatenate([blk, jnp.zeros((Q - rows, blk.shape[1]), F32)], axis=0)
    z = blk[:, OFF_Z:OFF_XBC]
    xbc = blk[:, OFF_XBC:OFF_DT]
    dt_raw = blk[:, OFF_DT:OFF_DT + H_B]

    ext = jnp.concatenate([carry_ref[...], xbc], axis=0)
    u = cb_ref[...] + xbc * cw_ref[CONV_K - 1:CONV_K, :]
    for s in range(1, CONV_K):
        u = u + pltpu.roll(ext, s, axis=0)[8:] * cw_ref[CONV_K - 1 - s:CONV_K - s, :]
    carry_ref[...] = xbc[Q - 8:]
    u = _silu(u)
    xs = u[:, :D_INNER]
    bmat = u[:, D_INNER:D_INNER + NG_B * D_STATE]
    cmat = u[:, D_INNER + NG_B * D_STATE:]

    row_q = lax.broadcasted_iota(I32, (Q, LANES), 0)
    lane = lax.broadcasted_iota(I32, (Q, LANES), 1)
    dt_pad = jnp.concatenate([dt_raw, jnp.zeros((Q, LANES - H_B), F32)], axis=1)
    dtv = jnp.where((lane < H_B) & (row_q < valid), jax.nn.softplus(dt_pad + dtb_ref[...]), 0.0)
    adt = dtv * (-jnp.exp(alog_ref[...]))
    li = lax.broadcasted_iota(I32, (Q, Q), 0)
    si = lax.broadcasted_iota(I32, (Q, Q), 1)
    causal = si <= li
    acum = _hi_dot(causal.astype(F32), adt)
    acum_t = acum.T
    a_last = acum[Q - 1:Q, :]

    expand = exp_ref[...]
    xdt = xs * _hi_dot(dtv, expand)
    in_decay = _hi_dot(jnp.exp(acum), expand)
    to_end = _hi_dot(jnp.exp(a_last - acum), expand)
    chunk_decay = in_decay[Q - 1:Q, :]

    state = state_ref[...]
    state_b = state.astype(BF16)
    lo = lane < HD_B
    y_groups, new_state = [], []
    hpg = H_B // NG_B
    for g in range(NG_B):
        bg = bmat[:, g * D_STATE:(g + 1) * D_STATE]
        cg = cmat[:, g * D_STATE:(g + 1) * D_STATE].astype(BF16)
        cbm = _qk(cg, bg.astype(BF16))
        ch0, ch1 = g * hpg * HD_B, (g + 1) * hpg * HD_B
        pairs = []
        for pr in range(hpg // 2):
            col = ch0 + pr * LANES
            xpair = xdt[:, col:col + LANES].astype(BF16)
            ys = []
            for par in range(2):
                h = g * hpg + 2 * pr + par
                seg = acum[:, h:h + 1] - acum_t[h:h + 1, :]
                dec = jnp.exp(jnp.where(causal, seg, NEG))
                ys.append(jnp.dot((cbm * dec).astype(BF16), xpair, preferred_element_type=F32))
            pairs.append(jnp.where(lo, ys[0], ys[1]))
        y_off = jnp.dot(cg, state_b[:, ch0:ch1], preferred_element_type=F32)
        y_groups.append(jnp.concatenate(pairs, axis=1) + y_off * in_decay[:, ch0:ch1])
        xe = (xdt[:, ch0:ch1] * to_end[:, ch0:ch1]).astype(BF16)
        new_state.append(jnp.dot(bg.T.astype(BF16), xe, preferred_element_type=F32))
    y = jnp.concatenate(y_groups, axis=1) + xs * dskip_ref[...]
    state_ref[...] = state * chunk_decay + jnp.concatenate(new_state, axis=1)

    ug = y * _silu(z)
    gw = D_INNER // NG_B
    outs = []
    for g in range(NG_B):
        v = ug[:, g * gw:(g + 1) * gw]
        outs.append(v * lax.rsqrt(jnp.mean(v * v, axis=1, keepdims=True) + RMS_EPS))
    yn = jnp.concatenate(outs, axis=1) * nw_ref[...]
    y_ref[...] = yn[:rows]

    @pl.when(c == nc - 1)
    def _():
        hout_ref[...] = state_ref[...].T


def _ssd(mix, proj, carry0, h0, conv_w, conv_b, dt_bias, a_log, d_skip, norm_w, *, row0, nbatch, nc, rows, valid,
         h0_base=0):
    expand = jnp.asarray(np.concatenate([np.repeat(np.eye(H_B, dtype=np.float32), HD_B, axis=1),
                                         np.zeros((LANES - H_B, D_INNER), np.float32)], axis=0))
    pad_h = lambda v: jnp.concatenate([v, jnp.zeros((LANES - H_B,), F32)]).reshape(1, LANES)
    cw = jnp.concatenate([conv_w, jnp.zeros((8 - CONV_K, CONV_DIM), F32)], axis=0)
    rb0 = row0 // rows
    vec = lambda n: pl.BlockSpec((1, n), lambda b, c: (0, 0))
    y_new, h_new = pl.pallas_call(
        functools.partial(_ssd_kernel, rows=rows, valid=valid, nc=nc),
        grid=(nbatch, nc),
        in_specs=[pl.BlockSpec((rows, EVEN_IN), lambda b, c: (rb0 + b * nc + c, 0)),
                  pl.BlockSpec((None, 8, CONV_DIM), lambda b, c: (b, 0, 0)),
                  pl.BlockSpec((None, D_INNER, D_STATE), lambda b, c: (h0_base + b, 0, 0)),
                  pl.BlockSpec((8, CONV_DIM), lambda b, c: (0, 0)),
                  vec(CONV_DIM), vec(LANES), vec(LANES), vec(D_INNER), vec(D_INNER),
                  pl.BlockSpec((LANES, D_INNER), lambda b, c: (0, 0)),
                  pl.BlockSpec(memory_space=pl.ANY)],
        out_specs=[pl.BlockSpec((rows, D_INNER), lambda b, c: (rb0 + b * nc + c, 1)),
                   pl.BlockSpec((None, D_INNER, D_STATE), lambda b, c: (b, 0, 0))],
        out_shape=[jax.ShapeDtypeStruct(mix.shape, F32),
                   jax.ShapeDtypeStruct((nbatch, D_INNER, D_STATE), F32)],
        scratch_shapes=[pltpu.VMEM((8, CONV_DIM), F32), pltpu.VMEM((D_STATE, D_INNER), F32)],
        input_output_aliases={10: 0},
        compiler_params=_cparams(("parallel", "arbitrary")), name="ssd",
    )(proj, carry0, h0, cw, conv_b.reshape(1, -1), pad_h(dt_bias),
      pad_h(a_log), jnp.repeat(d_skip, HD_B).reshape(1, -1), norm_w.reshape(1, -1), expand, mix)
    return y_new, h_new.reshape(nbatch, H_B, HD_B, D_STATE)


def _attn_c_prompt_kernel(q0_ref, q1_ref, q2_ref, q3_ref, kvp_ref, kvc_ref, cos_ref, sin_ref, o_ref, lse_ref):
    first = pl.program_id(1) == 0
    cos, sin = cos_ref[...], sin_ref[...]
    kvp, kvc = kvp_ref[...], kvc_ref[...]
    G = H_C // KV_C
    mask = _band_mask(G * BLK, first)
    lane = lax.broadcasted_iota(I32, (BLK, LANES), 1)
    lse_tile = jnp.zeros((BLK, LANES), F32)
    for kvh, q_ref in enumerate((q0_ref, q1_ref, q2_ref, q3_ref)):
        ks = slice(kvh * HD_C, (kvh + 1) * HD_C)
        vs = slice(KVW_C + kvh * HD_C, KVW_C + (kvh + 1) * HD_C)
        kk = jnp.concatenate([kvp[:, ks], kvc[:, ks]], axis=0).astype(BF16)
        vv = jnp.concatenate([kvp[:, vs], kvc[:, vs]], axis=0).astype(BF16)
        qs = jnp.concatenate([_rope_chunk(q_ref[:, i * HD_C:(i + 1) * HD_C], cos, sin, HD_C) for i in range(G)],
                             axis=0).astype(BF16)
        o, lse = _softmax_pv(jnp.where(mask, _qk(qs, kk), NEG), vv)
        for i in range(G):
            h = kvh * G + i
            o_ref[:, h * HD_C:(h + 1) * HD_C] = o[i * BLK:(i + 1) * BLK]
            lse_tile = jnp.where(lane == h, lse[i * BLK:(i + 1) * BLK], lse_tile)
    lse_ref[...] = lse_tile


def _attn_c_prompt(proj, kv, cosq, sinq, g, B, L):
    T = proj.shape[0]
    nb = L // BLK
    G = H_C // KV_C
    qspec = lambda kvh: pl.BlockSpec((BLK, G * HD_C), lambda b, m: (b * nb + m, g * KV_C + kvh))
    return pl.pallas_call(
        _attn_c_prompt_kernel,
        grid=(B, nb),
        in_specs=[qspec(0), qspec(1), qspec(2), qspec(3),
                  pl.BlockSpec((BLK, 2 * KVW_C), lambda b, m: (b * nb + jnp.maximum(m - 1, 0), 0)),
                  pl.BlockSpec((BLK, 2 * KVW_C), lambda b, m: (b * nb + m, 0)),
                  pl.BlockSpec((BLK, LANES), lambda b, m: (m, 0)),
                  pl.BlockSpec((BLK, LANES), lambda b, m: (m, 0))],
        out_specs=[pl.BlockSpec((BLK, QW_C), lambda b, m: (b * nb + m, 0)),
                   pl.BlockSpec((BLK, LANES), lambda b, m: (b * nb + m, 0))],
        out_shape=[jax.ShapeDtypeStruct((T, QW_C), F32), jax.ShapeDtypeStruct((T, LANES), F32)],
        compiler_params=_cparams(("parallel", "parallel")), name="attn_c_prompt_d1",
    )(proj, proj, proj, proj, kv, kv, cosq, sinq)


def _attn_c_dilated_kernel(q0_ref, q1_ref, q2_ref, q3_ref, kp_ref, kc_ref, vp_ref, vc_ref, cos_ref, sin_ref,
                           o_ref, lse_ref, osc_ref, *, dil):
    first = pl.program_id(1) == 0
    kvh = pl.program_id(2)
    q_refs = (q0_ref, q1_ref, q2_ref, q3_ref)
    G = len(q_refs)
    mask = _band_mask(G * BLK, first)
    lane = lax.broadcasted_iota(I32, (BLK, LANES), 1)

    @pl.when(kvh == 0)
    def _():
        lse_ref[...] = jnp.zeros_like(lse_ref)

    for r in range(dil):
        rows = pl.ds(r, BLK, stride=dil)
        cos, sin = cos_ref[rows, :], sin_ref[rows, :]
        kk = jnp.concatenate([kp_ref[rows, :], kc_ref[rows, :]], axis=0).astype(BF16)
        vv = jnp.concatenate([vp_ref[rows, :], vc_ref[rows, :]], axis=0).astype(BF16)
        qs = jnp.concatenate([_rope_chunk(q_ref[rows, :], cos, sin, HD_C) for q_ref in q_refs], axis=0).astype(BF16)
        o, lse = _softmax_pv(jnp.where(mask, _qk(qs, kk), NEG), vv)
        lse_tile = lse_ref[rows, :]
        for i in range(G):
            osc_ref[i, rows, :] = o[i * BLK:(i + 1) * BLK]
            lse_tile = jnp.where(lane == kvh * G + i, lse[i * BLK:(i + 1) * BLK], lse_tile)
        lse_ref[rows, :] = lse_tile
    for i in range(G):
        o_ref[:, i * HD_C:(i + 1) * HD_C] = osc_ref[i]


def _attn_c_dilated(proj, kv, cosq, sinq, g, dil, B, L):
    T = proj.shape[0]
    R = BLK * dil
    nb = L // R
    G = H_C // KV_C
    vb = KVW_C // HD_C
    prev = lambda b, m: b * nb + jnp.maximum(m - 1, 0)
    qspec = lambda i: pl.BlockSpec((R, HD_C), lambda b, m, h: (b * nb + m, g * H_C + h * G + i))
    return pl.pallas_call(
        functools.partial(_attn_c_dilated_kernel, dil=dil),
        grid=(B, nb, KV_C),
        in_specs=[qspec(0), qspec(1), qspec(2), qspec(3),
                  pl.BlockSpec((R, HD_C), lambda b, m, h: (prev(b, m), h)),
                  pl.BlockSpec((R, HD_C), lambda b, m, h: (b * nb + m, h)),
                  pl.BlockSpec((R, HD_C), lambda b, m, h: (prev(b, m), vb + h)),
                  pl.BlockSpec((R, HD_C), lambda b, m, h: (b * nb + m, vb + h)),
                  pl.BlockSpec((R, LANES), lambda b, m, h: (m, 0)),
                  pl.BlockSpec((R, LANES), lambda b, m, h: (m, 0))],
        out_specs=[pl.BlockSpec((R, G * HD_C), lambda b, m, h: (b * nb + m, h)),
                   pl.BlockSpec((R, LANES), lambda b, m, h: (b * nb + m, 0))],
        out_shape=[jax.ShapeDtypeStruct((T, QW_C), F32), jax.ShapeDtypeStruct((T, LANES), F32)],
        scratch_shapes=[pltpu.VMEM((G, R, HD_C), F32)],
        compiler_params=_cparams(("parallel", "parallel", "arbitrary")), name=f"attn_c_prompt_d{dil}",
    )(proj, proj, proj, proj, kv, kv, kv, kv, cosq, sinq)


def _mix_c_kernel(o0_ref, o1_ref, o2_ref, l0_ref, l1_ref, l2_ref, o_ref):
    ls = [l0_ref[...], l1_ref[...], l2_ref[...]]
    m = jnp.maximum(jnp.maximum(ls[0], ls[1]), ls[2])
    es = [jnp.exp(l - m) for l in ls]
    inv = 1.0 / (es[0] + es[1] + es[2])
    ws = [e * inv for e in es]
    for h in range(H_C):
        sl = slice(h * HD_C, (h + 1) * HD_C)
        o_ref[:, sl] = (ws[0][:, h:h + 1] * o0_ref[:, sl] + ws[1][:, h:h + 1] * o1_ref[:, sl]
                        + ws[2][:, h:h + 1] * o2_ref[:, sl])


def _mix_c(outs, lses, n_prompt):
    T = outs[0].shape[0]
    tm = BLK
    ospec = pl.BlockSpec((tm, QW_C), lambda i: (i, 0))
    lspec = pl.BlockSpec((tm, LANES), lambda i: (i, 0))
    return pl.pallas_call(
        _mix_c_kernel, grid=(n_prompt // tm,),
        in_specs=[ospec] * 3 + [lspec] * 3, out_specs=ospec,
        out_shape=jax.ShapeDtypeStruct((T, QW_C), F32),
        compiler_params=_cparams(("parallel",)), name="mix_c",
    )(*outs, *lses)


def _attn_c_sample_kernel(q_ref, kvn_ref, ck_ref, cv_ref, cos_ref, sin_ref, mix_ref, o_ref, *, lb):
    del mix_ref
    R = SAMPLE_PAD
    G = H_C // KV_C
    cos, sin = cos_ref[...], sin_ref[...]
    kvn = kvn_ref[...]
    zpad = jnp.zeros((BLK - R, HD_C), F32)
    nrow = N_DIL * G * R
    rho = lax.broadcasted_iota(I32, (nrow, 1), 0)
    t = rho & (R - 1)
    grp = rho // (G * R)
    dil_m1 = jnp.where(grp == 0, DILATIONS[0][1] - 1, jnp.where(grp == 1, DILATIONS[1][1] - 1, DILATIONS[2][1] - 1))
    win = jnp.where(grp == 0, DILATIONS[0][0], jnp.where(grp == 1, DILATIONS[1][0], DILATIONS[2][0]))
    jc = lax.broadcasted_iota(I32, (nrow, lb), 1)
    dc = lb + t - jc
    mask_c = ((dc & dil_m1) == 0) & (dc <= win)
    jn = lax.broadcasted_iota(I32, (nrow, BLK), 1)
    dn = t - jn
    mask_n = (dn >= 0) & ((dn & dil_m1) == 0) & (jn < R)
    for kvh in range(KV_C):
        ks = slice(kvh * HD_C, (kvh + 1) * HD_C)
        vs = slice(KVW_C + kvh * HD_C, KVW_C + (kvh + 1) * HD_C)
        kc = ck_ref[pl.ds(kvh, lb, stride=KV_C), :].astype(BF16)
        vc = cv_ref[pl.ds(kvh, lb, stride=KV_C), :].astype(BF16)
        kn = jnp.concatenate([kvn[:, ks], zpad], axis=0).astype(BF16)
        vn = jnp.concatenate([kvn[:, vs], zpad], axis=0).astype(BF16)
        parts = []
        for g in range(N_DIL):
            for i in range(G):
                c0 = g * QW_C + (kvh * G + i) * HD_C
                parts.append(_rope_chunk(q_ref[:, c0:c0 + HD_C], cos, sin, HD_C))
        qs = jnp.concatenate(parts, axis=0).astype(BF16)
        sc = jnp.where(mask_c, _qk(qs, kc), NEG)
        sn = jnp.where(mask_n, _qk(qs, kn), NEG)
        m = jnp.maximum(jnp.max(sc, axis=1, keepdims=True), jnp.max(sn, axis=1, keepdims=True))
        ec, en = jnp.exp(sc - m), jnp.exp(sn - m)
        den = jnp.sum(ec, axis=1, keepdims=True) + jnp.sum(en, axis=1, keepdims=True)
        o = (jnp.dot(ec.astype(BF16), vc, preferred_element_type=F32)
             + jnp.dot(en.astype(BF16), vn, preferred_element_type=F32)) / den
        lse = m + jnp.log(den)
        gr = G * R
        lg = [lse[g * gr:(g + 1) * gr] for g in range(N_DIL)]
        mm = jnp.maximum(jnp.maximum(lg[0], lg[1]), lg[2])
        eg = [jnp.exp(l - mm) for l in lg]
        inv = 1.0 / (eg[0] + eg[1] + eg[2])
        om = sum((eg[g] * inv) * o[g * gr:(g + 1) * gr] for g in range(N_DIL))
        for i in range(G):
            h = kvh * G + i
            o_ref[:, h * HD_C:(h + 1) * HD_C] = om[i * R:(i + 1) * R]


def _attn_c_sample(mix, proj, kv, cache_k, cache_v, layer, cosq, sinq, n_prompt, DB):
    r0 = n_prompt // SAMPLE_PAD
    lb = cache_k.shape[2]
    cspec = pl.BlockSpec((None, None, lb * KV_C, HD_C), lambda b: (layer, b, 0, 0))
    as_rows = lambda c: c.reshape(c.shape[0], DB, lb * KV_C, HD_C)
    return pl.pallas_call(
        functools.partial(_attn_c_sample_kernel, lb=lb),
        grid=(DB,),
        in_specs=[pl.BlockSpec((SAMPLE_PAD, ODD_IN), lambda b: (r0 + b, 0)),
                  pl.BlockSpec((SAMPLE_PAD, 2 * KVW_C), lambda b: (r0 + b, 0)),
                  cspec, cspec,
                  pl.BlockSpec((SAMPLE_PAD, LANES), lambda b: (0, 0)),
                  pl.BlockSpec((SAMPLE_PAD, LANES), lambda b: (0, 0)),
                  pl.BlockSpec(memory_space=pl.ANY)],
        out_specs=pl.BlockSpec((SAMPLE_PAD, QW_C), lambda b: (r0 + b, 0)),
        out_shape=jax.ShapeDtypeStruct(mix.shape, F32),
        input_output_aliases={6: 0},
        compiler_params=_cparams(("parallel",)), name="attn_c_sample",
    )(proj, kv, as_rows(cache_k), as_rows(cache_v), cosq, sinq, mix)


def _router_kernel(x_ref, rwh_ref, rwl_ref, rb_ref, idx_ref, gate_ref, rank_ref, cnt_ref, carry_ref):
    tm = x_ref.shape[0]

    @pl.when(pl.program_id(0) == 0)
    def _():
        carry_ref[...] = jnp.zeros_like(carry_ref)

    ninf = -jnp.inf
    x = x_ref[...]
    xh = x.astype(BF16)
    xl = (x - xh.astype(F32)).astype(BF16)
    wh = rwh_ref[...]
    logits = (jnp.dot(xh, wh, preferred_element_type=F32) + jnp.dot(xl, wh, preferred_element_type=F32)
              + jnp.dot(xh, rwl_ref[...], preferred_element_type=F32))
    pg = N_EXPERTS // N_EXPERT_GROUPS
    assert pg == 8 and TOP_K == 8 and N_EXPERT_GROUPS == 8
    scores = jax.nn.sigmoid(logits.T[:N_EXPERTS])
    choice = scores + jnp.concatenate([rb_ref[...]] * (tm // LANES), axis=1)
    sub = lax.broadcasted_iota(I32, (pg, tm), 0)
    smax = lambda v: jnp.max(v, axis=0, keepdims=True)
    smin = lambda v: jnp.min(v, axis=0, keepdims=True)
    ssum = lambda v: jnp.sum(v, axis=0, keepdims=True)
    cg = [choice[g * pg:(g + 1) * pg] for g in range(N_EXPERT_GROUPS)]
    sg = [scores[g * pg:(g + 1) * pg] for g in range(N_EXPERT_GROUPS)]
    gs = jnp.full((N_EXPERT_GROUPS, tm), ninf, F32)
    for g in range(N_EXPERT_GROUPS):
        m1 = smax(cg[g])
        i1 = smin(jnp.where(cg[g] == m1, sub, pg))
        m2 = smax(jnp.where(sub == i1, ninf, cg[g]))
        gs = jnp.where(sub == g, m1 + m2, gs)
    gmask = jnp.zeros((N_EXPERT_GROUPS, tm), jnp.bool_)
    for _ in range(TOPK_GROUPS):
        m = smax(gs)
        hit = sub == smin(jnp.where(gs == m, sub, N_EXPERT_GROUPS))
        gmask = gmask | hit
        gs = jnp.where(hit, ninf, gs)
    gsel = jnp.where(gmask, 1.0, 0.0)
    sel = [jnp.where(ssum(jnp.where(sub == g, gsel, 0.0)) > 0.5, cg[g], ninf) for g in range(N_EXPERT_GROUPS)]
    eid = [sub + g * pg for g in range(N_EXPERT_GROUPS)]
    tree = lambda op, xs: functools.reduce(op, xs)
    hots, idxs, gates = [], [], []
    for _ in range(TOP_K):
        m = smax(tree(jnp.maximum, sel))
        ik = smin(tree(jnp.minimum, [jnp.where(sel[g] == m, eid[g], N_EXPERTS) for g in range(N_EXPERT_GROUPS)]))
        hot = [eid[g] == ik for g in range(N_EXPERT_GROUPS)]
        gates.append(ssum(tree(jnp.add, [jnp.where(hot[g], sg[g], 0.0) for g in range(N_EXPERT_GROUPS)])))
        sel = [jnp.where(hot[g], ninf, sel[g]) for g in range(N_EXPERT_GROUPS)]
        hots.append(hot)
        idxs.append(ik)
    gsum = tree(jnp.add, gates)
    onehot = jnp.concatenate([tree(jnp.add, [hots[k][g].astype(F32) for k in range(TOP_K)])
                              for g in range(N_EXPERT_GROUPS)], axis=0)
    ri = lax.broadcasted_iota(I32, (tm, tm), 0)
    ci = lax.broadcasted_iota(I32, (tm, tm), 1)
    before = jnp.dot(onehot.astype(BF16), (ri < ci).astype(BF16), preferred_element_type=F32)
    base = before + jnp.concatenate([carry_ref[...]] * (tm // LANES), axis=1)
    idx_o = jnp.zeros((TOP_K, tm), I32)
    gate_o = jnp.zeros((TOP_K, tm), F32)
    rank_o = jnp.zeros((TOP_K, tm), I32)
    for k in range(TOP_K):
        rk = ssum(tree(jnp.add, [jnp.where(hots[k][g], base[g * pg:(g + 1) * pg], 0.0)
                                 for g in range(N_EXPERT_GROUPS)])).astype(I32)
        idx_o = jnp.where(sub == k, idxs[k], idx_o)
        gate_o = jnp.where(sub == k, gates[k] / gsum * ROUTED_SCALE, gate_o)
        rank_o = jnp.where(sub == k, rk, rank_o)
    idx_ref[...] = idx_o
    gate_ref[...] = gate_o
    rank_ref[...] = rank_o
    total = carry_ref[...] + jnp.sum(onehot, axis=1, keepdims=True)
    carry_ref[...] = total
    cnt_ref[...] = total


def _router(x, router_w, router_bias):
    T, D = x.shape
    tm = _pick(T, (256, 128))
    rw = jnp.concatenate([router_w, jnp.zeros((D, LANES - N_EXPERTS), F32)], axis=1)
    rwh = rw.astype(BF16)
    rwl = (rw - rwh.astype(F32)).astype(BF16)
    rb = jnp.broadcast_to(router_bias.astype(F32)[:, None], (N_EXPERTS, LANES))
    small = lambda: pl.BlockSpec((TOP_K, tm), lambda i: (0, i))
    return pl.pallas_call(
        _router_kernel, grid=(T // tm,),
        in_specs=[pl.BlockSpec((tm, D), lambda i: (i, 0)),
                  pl.BlockSpec((D, LANES), lambda i: (0, 0)),
                  pl.BlockSpec((D, LANES), lambda i: (0, 0)),
                  pl.BlockSpec((N_EXPERTS, LANES), lambda i: (0, 0))],
        out_specs=[small(), small(), small(), pl.BlockSpec((N_EXPERTS, LANES), lambda i: (0, 0))],
        out_shape=[jax.ShapeDtypeStruct((TOP_K, T), I32), jax.ShapeDtypeStruct((TOP_K, T), F32),
                   jax.ShapeDtypeStruct((TOP_K, T), I32), jax.ShapeDtypeStruct((N_EXPERTS, LANES), F32)],
        scratch_shapes=[pltpu.VMEM((N_EXPERTS, LANES), F32)],
        compiler_params=_cparams(("arbitrary",)), name="router",
    )(x, rwh, rwl, rb)


def _pack_bf16_pairs(x):
    half = x.shape[1] // 2
    lo = pltpu.bitcast(x[:, :half].astype(BF16).astype(F32), U32) >> 16
    hi = pltpu.bitcast(x[:, half:].astype(BF16).astype(F32), U32) & jnp.uint32(0xFFFF0000)
    return hi | lo


def _unpack_bf16_pairs(w):
    return pltpu.bitcast(w << 16, F32), pltpu.bitcast(w & jnp.uint32(0xFFFF0000), F32)


ROW_TILE = 8
OUT_TILE = 16


def _to_token_tiles(dst_ref, rows, n, rpt):
    for c in range(rpt):
        dst_ref[pl.ds(c, n, stride=rpt), :] = rows[:, c * LANES:(c + 1) * LANES]


def _from_token_tiles(load, n, rpt):
    return jnp.concatenate([load(pl.ds(c, n, stride=rpt)) for c in range(rpt)], axis=1)


def _tile_rows(slot, rpt=ROW_TILE):
    return pl.ds(pl.multiple_of(slot * rpt, rpt), rpt)


def _dispatch_kernel(zstart_ref, zcount_ref, x_ref, pos_hbm, xs_hbm, tiles_ref, ztile_ref, pos_ref, psem, sem):
    i = pl.program_id(0)
    tm = x_ref.shape[0]
    pcopy = pltpu.make_async_copy(pos_hbm.at[i], pos_ref, psem)
    pcopy.start()
    _to_token_tiles(tiles_ref, _pack_bf16_pairs(x_ref[...]), tm, ROW_TILE)

    def tile_copy(src, r, slot):
        return pltpu.make_async_copy(src.at[_tile_rows(r)], xs_hbm.at[_tile_rows(slot)], sem)

    @pl.when(i == 0)
    def _():
        ztile_ref[...] = jnp.zeros_like(ztile_ref)

        def per_expert(e, carry):
            def start(r, c):
                tile_copy(ztile_ref, 0, zstart_ref[e] + r).start()
                return c

            def wait(r, c):
                tile_copy(ztile_ref, 0, 0).wait()
                return c

            lax.fori_loop(0, zcount_ref[e], start, 0)
            lax.fori_loop(0, zcount_ref[e], wait, 0)
            return carry

        lax.fori_loop(0, N_EXPERTS, per_expert, 0)

    pcopy.wait()

    def start(r, c):
        for k in range(TOP_K):
            tile_copy(tiles_ref, r, pos_ref[k * tm + r]).start()
        return c

    def wait(r, c):
        for k in range(TOP_K):
            tile_copy(tiles_ref, 0, 0).wait()
        return c

    lax.fori_loop(0, tm, start, 0)
    lax.fori_loop(0, tm, wait, 0)


def _tile_major(a, tm):
    K, T = a.shape
    return a.reshape(K, T // tm, tm).transpose(1, 0, 2).reshape(T // tm, K * tm)


def _dispatch(x, pos, zstart, zcount, n_slots):
    T, D = x.shape
    assert D // 2 == ROW_TILE * LANES
    tm = _pick(T, (256, 128))
    return pl.pallas_call(
        _dispatch_kernel,
        grid_spec=pltpu.PrefetchScalarGridSpec(
            num_scalar_prefetch=2, grid=(T // tm,),
            in_specs=[pl.BlockSpec((tm, D), lambda i, *_: (i, 0)),
                      pl.BlockSpec(memory_space=pl.ANY)],
            out_specs=pl.BlockSpec(memory_space=pl.ANY),
            scratch_shapes=[pltpu.VMEM((tm * ROW_TILE, LANES), U32), pltpu.VMEM((ROW_TILE, LANES), U32),
                            pltpu.SMEM((tm * TOP_K,), I32),
                            pltpu.SemaphoreType.DMA, pltpu.SemaphoreType.DMA]),
        out_shape=jax.ShapeDtypeStruct((n_slots * ROW_TILE, LANES), U32),
        compiler_params=_cparams(("arbitrary",)),
        name="moe_dispatch",
    )(zstart, zcount, x, _tile_major(pos, tm))


def _experts_kernel(te_ref, nu_ref, ne_ref, sl_ref, xs_ref, w1_hbm, w3_hbm, w2_hbm, ys_ref,
                    wf1, wf3, wf2, w1b, w3b, w2b, sems, *, layer):
    i = pl.program_id(0)
    te = EXPERT_TILE

    def weight_copies(e, s):
        return (pltpu.make_async_copy(w1_hbm.at[layer, e], wf1.at[s], sems.at[s, 0]),
                pltpu.make_async_copy(w3_hbm.at[layer, e], wf3.at[s], sems.at[s, 1]),
                pltpu.make_async_copy(w2_hbm.at[layer, e], wf2.at[s], sems.at[s, 2]))

    @pl.when(i < nu_ref[0])
    def _():
        e, s = te_ref[i], sl_ref[i]

        @pl.when(i == 0)
        def _():
            for cp in weight_copies(e, s):
                cp.start()

        @pl.when((i == 0) | (e != te_ref[jnp.maximum(i - 1, 0)]))
        def _():
            for cp in weight_copies(e, s):
                cp.wait()
            w1b[...] = wf1[s].astype(BF16)
            w3b[...] = wf3[s].astype(BF16)
            w2b[...] = wf2[s].astype(BF16)

            @pl.when(ne_ref[i] >= 0)
            def _():
                for cp in weight_copies(ne_ref[i], 1 - s):
                    cp.start()

        lo, hi = _unpack_bf16_pairs(_from_token_tiles(lambda rows: xs_ref[rows, :], te, ROW_TILE))
        lo, hi = lo.astype(BF16), hi.astype(BF16)
        half = lo.shape[1]
        h1 = (jnp.dot(lo, w1b[:half, :], preferred_element_type=F32)
              + jnp.dot(hi, w1b[half:, :], preferred_element_type=F32))
        h3 = (jnp.dot(lo, w3b[:half, :], preferred_element_type=F32)
              + jnp.dot(hi, w3b[half:, :], preferred_element_type=F32))
        y = jnp.dot((_silu(h1) * h3).astype(BF16), w2b[...], preferred_element_type=F32)
        ys_ref[...] = y


def _experts(xs, tile_e, n_used, next_e, slot, w1, w3, w2, layer):
    te = EXPERT_TILE
    n_slots = xs.shape[0] // ROW_TILE
    nt = n_slots // te
    D, DE = w1.shape[2], w1.shape[3]
    assert D == OUT_TILE * LANES
    live = lambda i, te_, nu, *_: (jnp.minimum(i, nu[0] - 1), 0)
    anyspec = pl.BlockSpec(memory_space=pl.ANY)
    return pl.pallas_call(
        functools.partial(_experts_kernel, layer=layer),
        grid_spec=pltpu.PrefetchScalarGridSpec(
            num_scalar_prefetch=4, grid=(nt,),
            in_specs=[pl.BlockSpec((te * ROW_TILE, LANES), live), anyspec, anyspec, anyspec],
            out_specs=pl.BlockSpec((te, D), live),
            scratch_shapes=[pltpu.VMEM((2, D, DE), F32), pltpu.VMEM((2, D, DE), F32), pltpu.VMEM((2, DE, D), F32),
                            pltpu.VMEM((D, DE), BF16), pltpu.VMEM((D, DE), BF16), pltpu.VMEM((DE, D), BF16),
                            pltpu.SemaphoreType.DMA((2, 3))]),
        out_shape=jax.ShapeDtypeStruct((n_slots, D), F32),
        compiler_params=_cparams(("arbitrary",)), name="moe_experts",
    )(tile_e, n_used, next_e, slot, xs, w1, w3, w2)


def _shared_kernel(x_ref, w1_ref, w3_ref, w2_ref, o_ref, w1b, w3b, w2b):
    @pl.when(pl.program_id(0) == 0)
    def _():
        w1b[...] = w1_ref[...].astype(BF16)
        w3b[...] = w3_ref[...].astype(BF16)
        w2b[...] = w2_ref[...].astype(BF16)

    xb = x_ref[...].astype(BF16)
    h1 = jnp.dot(xb, w1b[...], preferred_element_type=F32)
    h3 = jnp.dot(xb, w3b[...], preferred_element_type=F32)
    o_ref[...] = jnp.dot((_silu(h1) * h3).astype(BF16), w2b[...], preferred_element_type=F32)


def _shared_expert(x, w1, w3, w2, layer):
    T, D = x.shape
    DE = w1.shape[2]
    tm = _pick(T, (256, 128))
    const = lambda r, c: pl.BlockSpec((None, r, c), lambda i: (layer, 0, 0))
    return pl.pallas_call(
        _shared_kernel, grid=(T // tm,),
        in_specs=[pl.BlockSpec((tm, D), lambda i: (i, 0)), const(D, DE), const(D, DE), const(DE, D)],
        out_specs=pl.BlockSpec((tm, D), lambda i: (i, 0)),
        out_shape=jax.ShapeDtypeStruct((T, D), F32),
        scratch_shapes=[pltpu.VMEM((D, DE), BF16), pltpu.VMEM((D, DE), BF16), pltpu.VMEM((DE, D), BF16)],
        compiler_params=_cparams(("arbitrary",)), name="moe_shared",
    )(x, w1, w3, w2)


def _combine_kernel(x_ref, sh_ref, gate_ref, g_ref, b_ref, pos_hbm, ys_hbm, o_ref, buf_ref, pos_ref, psem, sem):
    i = pl.program_id(0)
    tm = x_ref.shape[0]
    pcopy = pltpu.make_async_copy(pos_hbm.at[i], pos_ref, psem)
    pcopy.start()
    pcopy.wait()

    def tile_copy(slot, k, r):
        return pltpu.make_async_copy(ys_hbm.at[pl.ds(slot, 1)], buf_ref.at[k, pl.ds(r, 1)], sem)

    def start(r, c):
        for k in range(TOP_K):
            tile_copy(pos_ref[k * tm + r], k, r).start()
        return c

    def wait(r, c):
        for k in range(TOP_K):
            tile_copy(0, 0, 0).wait()
        return c

    lax.fori_loop(0, tm, start, 0)
    lax.fori_loop(0, tm, wait, 0)
    gate = gate_ref[...]
    y = sh_ref[...]
    for k in range(TOP_K):
        y = y + gate[:, k:k + 1] * buf_ref[k]
    o_ref[...] = _layernorm_rows(ALPHA * x_ref[...] + y, g_ref[...], b_ref[...])


def _combine(x, shared, gate, pos, ys, g, b):
    T, D = x.shape
    tm = BLK
    row = pl.BlockSpec((tm, D), lambda i: (i, 0))
    vec = pl.BlockSpec((1, D), lambda i: (0, 0))
    return pl.pallas_call(
        _combine_kernel, grid=(T // tm,),
        in_specs=[row, row, pl.BlockSpec((tm, TOP_K), lambda i: (i, 0)), vec, vec,
                  pl.BlockSpec(memory_space=pl.ANY), pl.BlockSpec(memory_space=pl.ANY)],
        out_specs=row,
        out_shape=jax.ShapeDtypeStruct((T, D), F32),
        scratch_shapes=[pltpu.VMEM((TOP_K, tm, D), F32), pltpu.SMEM((tm * TOP_K,), I32),
                        pltpu.SemaphoreType.DMA, pltpu.SemaphoreType.DMA],
        compiler_params=_cparams(("arbitrary",)), name="moe_combine",
    )(x, shared, gate, g.reshape(1, D), b.reshape(1, D), _tile_major(pos, tm), ys)


def _moe_ln(x, g, b, router_w, router_bias, w1, w3, w2, w1s, w3s, w2s, layer):
    T = x.shape[0]
    idx, gate, rank, cnt = _router(x, router_w, router_bias)
    te = EXPERT_TILE
    counts = cnt[:, 0].astype(I32)
    padded = (counts + te - 1) // te * te
    pad_end = jnp.cumsum(padded)
    pad_start = pad_end - padded
    eids = jnp.arange(N_EXPERTS, dtype=I32)
    first_slot = jnp.sum(jnp.where(idx[:, :, None] == eids, pad_start.astype(I32), 0), axis=-1)
    pos = first_slot + rank
    nt = (T * TOP_K + N_EXPERTS * (te - 1)) // te + 1
    n_used = (pad_end[-1] // te).astype(I32)
    tiles = jnp.minimum(jnp.arange(nt, dtype=I32), n_used - 1)
    tile_e = jnp.minimum(jnp.sum((pad_end[None, :] <= (tiles * te)[:, None]).astype(I32), axis=1), N_EXPERTS - 1)
    used = padded > 0
    later_used = used[None, :] & (eids[None, :] > eids[:, None])
    next_used = jnp.min(jnp.where(later_used, eids[None, :], N_EXPERTS), axis=1)
    next_used = jnp.where(next_used < N_EXPERTS, next_used, -1).astype(I32)
    run_index = jnp.cumsum(used.astype(I32)) - 1
    xs = _dispatch(x, pos, (pad_start + counts).astype(I32), (padded - counts).astype(I32), nt * te)
    of_tile = lambda table: jnp.sum(jnp.where(tile_e[:, None] == eids, table, 0), axis=1).astype(I32)
    ys = _experts(xs, tile_e, n_used.reshape(1), of_tile(next_used), of_tile(run_index % 2), w1, w3, w2, layer)
    shared = _shared_expert(x, w1s, w3s, w2s, layer)
    return _combine(x, shared, gate.T, pos, ys, g, b)


def _ple_kernel(x_ref, xr_ref, p_ref, wg_ref, wp_ref, o_ref, xb_ref, pb_ref):
    @pl.when(pl.program_id(1) == 0)
    def _():
        xb_ref[...] = x_ref[...].astype(BF16)
        pb_ref[...] = p_ref[...].astype(BF16)

    gate = jax.nn.sigmoid(jnp.dot(xb_ref[...], wg_ref[...].astype(BF16), preferred_element_type=F32))
    proj = jnp.dot(pb_ref[...], wp_ref[...].astype(BF16), preferred_element_type=F32)
    o_ref[...] = xr_ref[...] + gate * proj


def _ple(x, p, wg, wp, layer, tn=512):
    T, D = x.shape
    P = p.shape[1]
    tm = _pick(T, (768, 512, 384, 256, 128))
    return pl.pallas_call(
        _ple_kernel, grid=(T // tm, D // tn),
        in_specs=[pl.BlockSpec((tm, D), lambda i, j: (i, 0)),
                  pl.BlockSpec((tm, tn), lambda i, j: (i, j)),
                  pl.BlockSpec((tm, P), lambda i, j: (i, 0)),
                  pl.BlockSpec((None, D, tn), lambda i, j: (layer, 0, j)),
                  pl.BlockSpec((None, P, tn), lambda i, j: (layer, 0, j))],
        out_specs=pl.BlockSpec((tm, tn), lambda i, j: (i, j)),
        out_shape=jax.ShapeDtypeStruct((T, D), F32),
        scratch_shapes=[pltpu.VMEM((tm, D), BF16), pltpu.VMEM((tm, P), BF16)],
        compiler_params=_cparams(("parallel", "arbitrary")), name="ple",
    )(x, x, p, wg, wp)


def _pad_time(a, axis=1):
    pad = [(0, 0)] * a.ndim
    pad[axis] = (0, SAMPLE_PAD - a.shape[axis])
    return jnp.pad(a, pad)


def kernel(x_prompt, x_sample, cache_a_k, cache_a_v, state_b_ssm, state_b_conv, cache_c_k, cache_c_v, p_prompt, p_sample, w_in_even, sink_a, conv_w_b, conv_b_b, dt_bias_b, a_log_b, d_skip_b, norm_w_b, w_out_even, w_in_odd, w_out_odd, ln_g, ln_b, router_w, router_bias, w1_e, w3_e, w2_e, w1_s, w3_s, w2_s, w_ple_gate, w_ple_proj):
    B, L, D = x_prompt.shape
    DB, TS, _ = x_sample.shape
    NP = B * L
    assert L % (BLK * max(d for _, d in DILATIONS)) == 0 and TS <= SAMPLE_PAD and CONV_K - 1 <= TS
    NS = DB * SAMPLE_PAD
    npb = NP // BLK

    x = jnp.concatenate([x_prompt.reshape(NP, D), _pad_time(x_sample).reshape(NS, D)], axis=0)

    prompt_pos = np.arange(L)
    sample_pos = PAST_LEN + np.arange(SAMPLE_PAD)
    all_pos = np.concatenate([prompt_pos, np.tile(sample_pos, BLK // SAMPLE_PAD)])
    tabs = {}
    for name, hd in (("a", HD_A), ("c", HD_C)):
        sc = hd ** -0.5
        tabs[name] = dict(k=_rope_tables(all_pos, hd), qp=_rope_tables(prompt_pos, hd, sc),
                          qs=_rope_tables(sample_pos, hd, sc))

    zeros_carry = jnp.zeros((B, 8, CONV_DIM), F32)
    zeros_state = jnp.zeros((B, D_INNER, D_STATE), F32)
    ssm_in = state_b_ssm.reshape(-1, D_INNER, D_STATE)

    outs = {k: [] for k in ("pa_k", "pa_v", "pb_ssm", "pb_conv", "pc_k", "pc_v",
                            "sa_k", "sa_v", "sb_ssm", "sb_conv", "sc_k", "sc_v")}
    for i in range(DEPTH):
        j = i // 2
        if i % 2 == 0:
            proj = _matmul(x, w_in_even, j)
            kv = _rope_kv(proj, *tabs["a"]["k"], npb, L // BLK, HD_A, KV_A * HD_A, OFF_K_A)
            mix = _attn_a_prompt(proj, kv, sink_a[j], *tabs["a"]["qp"], B, L)
            mix = _attn_a_sample(mix, proj, kv, cache_a_k[j], cache_a_v[j], sink_a[j], *tabs["a"]["qs"], NP, DB)
            ssm_w = (conv_w_b[j], conv_b_b[j], dt_bias_b[j], a_log_b[j], d_skip_b[j], norm_w_b[j])
            mix, hp = _ssd(mix, proj, zeros_carry, zeros_state, *ssm_w,
                           row0=0, nbatch=B, nc=L // BLK, rows=BLK, valid=BLK)
            carry_s = jnp.pad(state_b_conv[j], ((0, 0), (8 - (CONV_K - 1), 0), (0, 0)))
            mix, hs = _ssd(mix, proj, carry_s, ssm_in, *ssm_w,
                           row0=NP, nbatch=DB, nc=1, rows=SAMPLE_PAD, valid=TS, h0_base=j * DB)
            w_out = w_out_even
            kvp = kv[:NP].reshape(B, L, 2, KV_A, HD_A)
            kvs = kv[NP:].reshape(DB, SAMPLE_PAD, 2, KV_A, HD_A)
            keep = min(WINDOW_A, L)
            outs["pa_k"].append(kvp[:, L - keep:, 0]); outs["pa_v"].append(kvp[:, L - keep:, 1])
            outs["sa_k"].append(kvs[:, :TS, 0]); outs["sa_v"].append(kvs[:, :TS, 1])
            outs["pb_ssm"].append(hp.reshape(B, H_B, HD_B, D_STATE))
            outs["sb_ssm"].append(hs.reshape(DB, H_B, HD_B, D_STATE))
            outs["pb_conv"].append(proj[:NP].reshape(B, L, EVEN_IN)[:, L - (CONV_K - 1):, OFF_XBC:OFF_DT])
            outs["sb_conv"].append(proj[NP:].reshape(DB, SAMPLE_PAD, EVEN_IN)[:, TS - (CONV_K - 1):TS, OFF_XBC:OFF_DT])
        else:
            proj = _matmul(x, w_in_odd, j)
            kv, k_heads, v_heads = _rope_kv(proj, *tabs["c"]["k"], npb, L // BLK, HD_C, KVW_C, OFF_K_C, per_head=True)
            go, gl = [], []
            for g, (win, dil) in enumerate(DILATIONS):
                assert win // dil == BLK
                if dil == 1:
                    o_g, l_g = _attn_c_prompt(proj, kv, *tabs["c"]["qp"], g, B, L)
                else:
                    o_g, l_g = _attn_c_dilated(proj, kv, *tabs["c"]["qp"], g, dil, B, L)
                go.append(o_g); gl.append(l_g)
            mix = _mix_c(go, gl, NP)
            mix = _attn_c_sample(mix, proj, kv, cache_c_k, cache_c_v, j, *tabs["c"]["qs"], NP, DB)
            w_out = w_out_odd
            keep = min(max(w for w, _ in DILATIONS), L)
            for name, heads in (("k", k_heads), ("v", v_heads)):
                outs["pc_" + name].append(heads[:NP * KV_C].reshape(B, L, KV_C, HD_C)[:, L - keep:])
                outs["sc_" + name].append(heads[NP * KV_C:].reshape(DB, SAMPLE_PAD, KV_C, HD_C)[:, :TS])
        x = _matmul_add_ln(mix, w_out, j, x, ln_g[i, 0], ln_b[i, 0])
        x = _moe_ln(x, ln_g[i, 1], ln_b[i, 1], router_w[i], router_bias[i], w1_e, w3_e, w2_e,
                    w1_s, w3_s, w2_s, i)
        p = jnp.concatenate([p_prompt[i].reshape(NP, PLE_DIM), _pad_time(p_sample[i]).reshape(NS, PLE_DIM)], axis=0)
        x = _ple(x, p, w_ple_gate, w_ple_proj, i)

    st = lambda k: jnp.stack(outs[k])
    y_prompt = x[:NP].reshape(B, L, D)
    y_sample = x[NP:].reshape(DB, SAMPLE_PAD, D)[:, :TS]
    return (y_prompt, y_sample, st("pa_k"), st("pa_v"), st("pb_ssm"), st("pb_conv"), st("pc_k"), st("pc_v"),
            st("sa_k"), st("sa_v"), st("sb_ssm"), st("sb_conv"), st("sc_k"), st("sc_v"))
```
